```python
import jax
import jax.numpy as jnp
from jax import lax
import numpy as np

D_MODEL = 1024
BATCH = 2
SEQ = 8192
DEPTH = 2
DEC_BATCH = 128
DEC_SEQ = 4
PAST_LEN = 8192
PAGE_SIZE = 128

N_MIXERS = 2
N_SWA_LAYERS = (DEPTH + 1) // 2
N_MLA_LAYERS = DEPTH // 2

SWA_HEADS = 16
SWA_KV_HEADS = 4
SWA_GROUP = SWA_HEADS // SWA_KV_HEADS
SWA_HEAD_DIM = 64
WINDOW = 128
SWA_BLOCK = WINDOW

MLA_HEADS = 16
MLA_Q_LORA = 384
MLA_KV_LORA = 256
MLA_NOPE = 64
MLA_ROPE = 32
MLA_V = 64
MLA_Q_BLOCK = 128
ROPE_THETA = 10000.0

N_GROUPS = 8
EXPERTS_PER_GROUP = 8
N_EXPERTS = N_GROUPS * EXPERTS_PER_GROUP
TOP_K = 2
D_EXPERT = 256
MOE_BLOCK = 128

RMS_EPS = 1e-6
NEG_INF = -1e30

kernel_name = 'hybrid_swa_mla_hmoe_step'


def rms_norm(x, g):
    xf = x.astype(jnp.float32)
    y = xf * lax.rsqrt(jnp.mean(xf * xf, axis=-1, keepdims=True) + RMS_EPS)
    return (y * g.astype(jnp.float32)).astype(x.dtype)


def apply_rope(x, pos):
    half = MLA_ROPE // 2
    inv = jnp.power(ROPE_THETA, -jnp.arange(half, dtype=jnp.float32) * 2.0 / MLA_ROPE)
    ang = pos[:, None] * inv[None, :]
    shape = (1, ang.shape[0]) + (1,) * (x.ndim - 3) + (half,)
    cos = jnp.cos(ang).reshape(shape)
    sin = jnp.sin(ang).reshape(shape)
    xf = x.astype(jnp.float32)
    x1, x2 = xf[..., :half], xf[..., half:]
    return jnp.concatenate([x1 * cos - x2 * sin, x2 * cos + x1 * sin], axis=-1).astype(x.dtype)


def alibi_slopes():
    h = jnp.arange(1, SWA_HEADS + 1, dtype=jnp.float32)
    return jnp.exp2(-8.0 * h / SWA_HEADS).reshape(SWA_KV_HEADS, SWA_GROUP)


def sink_softmax(scores, sinks):
    m = jnp.maximum(jnp.max(scores, axis=-1, keepdims=True), sinks)
    p = jnp.exp(scores - m)
    return p / (jnp.sum(p, axis=-1, keepdims=True) + jnp.exp(sinks - m))


def swa_qkv(h, w_qkv, b_qkv):
    b, s, _ = h.shape
    nq = SWA_HEADS * SWA_HEAD_DIM
    nk = SWA_KV_HEADS * SWA_HEAD_DIM
    qkv = h @ w_qkv + b_qkv
    q = qkv[..., :nq].reshape(b, s, SWA_KV_HEADS, SWA_GROUP, SWA_HEAD_DIM)
    k = qkv[..., nq:nq + nk].reshape(b, s, SWA_KV_HEADS, SWA_HEAD_DIM)
    v = qkv[..., nq + nk:].reshape(b, s, SWA_KV_HEADS, SWA_HEAD_DIM)
    return q, k, v


def swa_prompt(h, w_qkv, b_qkv, sinks, w_o, b_o):
    b, s, _ = h.shape
    nb = s // SWA_BLOCK
    scale = SWA_HEAD_DIM ** -0.5
    q, k, v = swa_qkv(h, w_qkv, b_qkv)
    qb = q.reshape(b, nb, SWA_BLOCK, SWA_KV_HEADS, SWA_GROUP, SWA_HEAD_DIM)
    kb = k.reshape(b, nb, SWA_BLOCK, SWA_KV_HEADS, SWA_HEAD_DIM)
    vb = v.reshape(b, nb, SWA_BLOCK, SWA_KV_HEADS, SWA_HEAD_DIM)
    pad = jnp.zeros_like(kb[:, :1])
    kk = jnp.concatenate([jnp.concatenate([pad, kb[:, :-1]], axis=1), kb], axis=2)
    vv = jnp.concatenate([jnp.concatenate([pad, vb[:, :-1]], axis=1), vb], axis=2)
    sc = jnp.einsum('bnqkgd,bnskd->bnkgqs', qb, kk, preferred_element_type=jnp.float32) * scale
    qi = jnp.arange(SWA_BLOCK)[:, None]
    kj = jnp.arange(2 * SWA_BLOCK)[None, :]
    dist = qi + SWA_BLOCK - kj
    kpos = jnp.arange(nb)[:, None, None] * SWA_BLOCK + kj[None] - SWA_BLOCK
    valid = (dist >= 0) & (dist < WINDOW) & (kpos >= 0)
    bias = -alibi_slopes()[:, :, None, None] * dist.astype(jnp.float32)
    sc = jnp.where(valid[None, :, None, None], sc + bias, NEG_INF)
    p = sink_softmax(sc, sinks.astype(jnp.float32).reshape(SWA_KV_HEADS, SWA_GROUP)[:, :, None, None])
    o = jnp.einsum('bnkgqs,bnskd->bnqkgd', p.astype(vv.dtype), vv)
    y = o.reshape(b, s, SWA_HEADS * SWA_HEAD_DIM) @ w_o + b_o
    return y, k[:, s - WINDOW:], v[:, s - WINDOW:]


def swa_sample(h, buf_k, buf_v, w_qkv, b_qkv, sinks, w_o, b_o):
    b, t, _ = h.shape
    scale = SWA_HEAD_DIM ** -0.5
    q, k, v = swa_qkv(h, w_qkv, b_qkv)
    kk = jnp.concatenate([buf_k, k], axis=1)
    vv = jnp.concatenate([buf_v, v], axis=1)
    sc = jnp.einsum('btkgd,bskd->bkgts', q, kk, preferred_element_type=jnp.float32) * scale
    dist = (WINDOW + jnp.arange(t))[:, None] - jnp.arange(WINDOW + t)[None, :]
    valid = (dist >= 0) & (dist < WINDOW)
    bias = -alibi_slopes()[:, :, None, None] * dist.astype(jnp.float32)
    sc = jnp.where(valid, sc + bias, NEG_INF)
    p = sink_softmax(sc, sinks.astype(jnp.float32).reshape(SWA_KV_HEADS, SWA_GROUP)[:, :, None, None])
    o = jnp.einsum('bkgts,bskd->btkgd', p.astype(vv.dtype), vv)
    y = o.reshape(b, t, SWA_HEADS * SWA_HEAD_DIM) @ w_o + b_o
    return y, kk[:, t:], vv[:, t:]


def mla_project(h, pos, w_dqkv, norm_q, norm_kv, w_uq):
    b, s, _ = h.shape
    a = h @ w_dqkv
    cq = rms_norm(a[..., :MLA_Q_LORA], norm_q)
    c = rms_norm(a[..., MLA_Q_LORA:MLA_Q_LORA + MLA_KV_LORA], norm_kv)
    kr = apply_rope(a[..., MLA_Q_LORA + MLA_KV_LORA:], pos)
    q = (cq @ w_uq).reshape(b, s, MLA_HEADS, MLA_NOPE + MLA_ROPE)
    qn = q[..., :MLA_NOPE]
    qr = apply_rope(q[..., MLA_NOPE:], pos)
    return qn, qr, c, kr


def mla_prompt(h, w_dqkv, norm_q, norm_kv, w_uq, w_uk, w_uv, w_o):
    b, s, _ = h.shape
    scale = (MLA_NOPE + MLA_ROPE) ** -0.5
    qn, qr, c, kr = mla_project(h, jnp.arange(s, dtype=jnp.float32), w_dqkv, norm_q, norm_kv, w_uq)
    kn = jnp.einsum('bsc,chd->bshd', c, w_uk)
    v = jnp.einsum('bsc,chd->bshd', c, w_uv)
    nb = s // MLA_Q_BLOCK
    qn_b = qn.reshape(b, nb, MLA_Q_BLOCK, MLA_HEADS, MLA_NOPE).transpose(1, 0, 2, 3, 4)
    qr_b = qr.reshape(b, nb, MLA_Q_BLOCK, MLA_HEADS, MLA_ROPE).transpose(1, 0, 2, 3, 4)
    kpos = jnp.arange(s)

    def q_block(args):
        qn_i, qr_i, n = args
        sc = (jnp.einsum('bqhd,bshd->bhqs', qn_i, kn, preferred_element_type=jnp.float32)
              + jnp.einsum('bqhr,bsr->bhqs', qr_i, kr, preferred_element_type=jnp.float32)) * scale
        qpos = n * MLA_Q_BLOCK + jnp.arange(MLA_Q_BLOCK)
        sc = jnp.where(kpos[None, :] <= qpos[:, None], sc, NEG_INF)
        p = jax.nn.softmax(sc, axis=-1)
        return jnp.einsum('bhqs,bshd->bqhd', p.astype(v.dtype), v)

    o = lax.map(q_block, (qn_b, qr_b, jnp.arange(nb)))
    y = o.transpose(1, 0, 2, 3, 4).reshape(b, s, MLA_HEADS * MLA_V) @ w_o
    return y, c, kr


def mla_sample(h, cache_c, cache_r, page_table, w_dqkv, norm_q, norm_kv, w_uq, w_uk, w_uv, w_o):
    b, t, _ = h.shape
    scale = (MLA_NOPE + MLA_ROPE) ** -0.5
    pos = PAST_LEN + jnp.arange(t, dtype=jnp.float32)
    qn, qr, c, kr = mla_project(h, pos, w_dqkv, norm_q, norm_kv, w_uq)
    past_c = cache_c[page_table].reshape(b, -1, MLA_KV_LORA)
    past_r = cache_r[page_table].reshape(b, -1, MLA_ROPE)
    n_past = past_c.shape[1]
    q_lat = jnp.einsum('bthd,chd->bthc', qn, w_uk)
    s_past = (jnp.einsum('bthc,bsc->bhts', q_lat, past_c, preferred_element_type=jnp.float32)
              + jnp.einsum('bthr,bsr->bhts', qr, past_r, preferred_element_type=jnp.float32)) * scale
    s_new = (jnp.einsum('bthc,bsc->bhts', q_lat, c, preferred_element_type=jnp.float32)
             + jnp.einsum('bthr,bsr->bhts', qr, kr, preferred_element_type=jnp.float32)) * scale
    causal = jnp.arange(t)[None, :] <= jnp.arange(t)[:, None]
    s_new = jnp.where(causal, s_new, NEG_INF)
    p = jax.nn.softmax(jnp.concatenate([s_past, s_new], axis=-1), axis=-1).astype(c.dtype)
    o_lat = (jnp.einsum('bhts,bsc->bthc', p[..., :n_past], past_c)
             + jnp.einsum('bhts,bsc->bthc', p[..., n_past:], c))
    o = jnp.einsum('bthc,chd->bthd', o_lat, w_uv)
    y = o.reshape(b, t, MLA_HEADS * MLA_V) @ w_o
    return y, c, kr


def hier_moe(x, w_group, b_group, w_expert, b_expert, w_gate, w_up, w_down):
    n_tok, d = x.shape
    xf = x.astype(jnp.float32)
    g_logits = xf @ w_group.astype(jnp.float32) + b_group.astype(jnp.float32)
    g_prob = jax.nn.softmax(g_logits, axis=-1)
    g_idx = jnp.argmax(g_logits, axis=-1)
    g_gate = jnp.take_along_axis(g_prob, g_idx[:, None], axis=-1)
    e_logits = (xf @ w_expert.astype(jnp.float32) + b_expert.astype(jnp.float32)).reshape(n_tok, N_GROUPS, EXPERTS_PER_GROUP)
    e_in = jnp.take_along_axis(e_logits, g_idx[:, None, None], axis=1)[:, 0]
    e_val, e_idx = lax.top_k(e_in, TOP_K)
    gates = jax.nn.softmax(e_val, axis=-1) * g_gate
    expert_id = g_idx[:, None] * EXPERTS_PER_GROUP + e_idx
    n_asg = n_tok * TOP_K
    flat_e = expert_id.reshape(n_asg)
    order = jnp.argsort(flat_e)
    sorted_e = flat_e[order]
    tok = order // TOP_K
    counts = jnp.bincount(flat_e, length=N_EXPERTS)
    padded = (counts + MOE_BLOCK - 1) // MOE_BLOCK * MOE_BLOCK
    start = jnp.cumsum(counts) - counts
    pend = jnp.cumsum(padded)
    pstart = pend - padded
    dest = pstart[sorted_e] + jnp.arange(n_asg) - start[sorted_e]
    n_blocks = -(-n_asg // MOE_BLOCK) + N_EXPERTS
    buf = jnp.zeros((n_blocks * MOE_BLOCK, d), x.dtype).at[dest].set(x[tok])
    blk_e = jnp.minimum(jnp.searchsorted(pend, jnp.arange(n_blocks) * MOE_BLOCK, side='right'), N_EXPERTS - 1)

    def expert_block(args):
        xb, e = args
        hb = jax.nn.silu(xb @ w_gate[e]) * (xb @ w_up[e])
        return hb @ w_down[e]

    yb = lax.map(expert_block, (buf.reshape(n_blocks, MOE_BLOCK, d), blk_e)).reshape(n_blocks * MOE_BLOCK, d)
    y_sorted = yb[dest] * gates.reshape(n_asg)[order][:, None].astype(x.dtype)
    return jnp.zeros_like(x).at[tok].add(y_sorted)


def setup_inputs(seed: int = 0) -> dict:
    key = jax.random.key(seed)
    ks = jax.random.split(key, 32)
    f32 = jnp.float32
    n_pages = PAST_LEN // PAGE_SIZE
    n_used = DEC_BATCH * n_pages
    n_pool = n_used + max(1, n_used // 4)

    def nrm(k, shape, scale):
        return jax.random.normal(k, shape, f32) * scale

    def gain(k, shape):
        return 1.0 + nrm(k, shape, 0.05)

    qkv_width = (SWA_HEADS + 2 * SWA_KV_HEADS) * SWA_HEAD_DIM
    dqkv_width = MLA_Q_LORA + MLA_KV_LORA + MLA_ROPE
    page_table = jax.random.permutation(ks[6], n_pool)[:n_used].reshape(DEC_BATCH, n_pages).astype(jnp.int32)
    return {
        'x_prompt': nrm(ks[0], (BATCH, SEQ, D_MODEL), 1.0),
        'x_sample': nrm(ks[1], (DEC_BATCH, DEC_SEQ, D_MODEL), 1.0),
        'cache_swa_k': nrm(ks[2], (N_SWA_LAYERS, DEC_BATCH, WINDOW, SWA_KV_HEADS, SWA_HEAD_DIM), 1.0),
        'cache_swa_v': nrm(ks[3], (N_SWA_LAYERS, DEC_BATCH, WINDOW, SWA_KV_HEADS, SWA_HEAD_DIM), 1.0),
        'cache_mla_latent': nrm(ks[4], (N_MLA_LAYERS, n_pool, PAGE_SIZE, MLA_KV_LORA), 1.0),
        'cache_mla_krope': nrm(ks[5], (N_MLA_LAYERS, n_pool, PAGE_SIZE, MLA_ROPE), 1.0),
        'page_table': page_table,
        'norm_attn': gain(ks[7], (DEPTH, D_MODEL)),
        'norm_ffn': gain(ks[8], (DEPTH, D_MODEL)),
        'norm_final': gain(ks[9], (D_MODEL,)),
        'swa_w_qkv': nrm(ks[10], (N_SWA_LAYERS, D_MODEL, qkv_width), D_MODEL ** -0.5),
        'swa_b_qkv': nrm(ks[11], (N_SWA_LAYERS, qkv_width), 0.02),
        'swa_sinks': nrm(ks[12], (N_SWA_LAYERS, SWA_HEADS), 0.5),
        'swa_w_o': nrm(ks[13], (N_SWA_LAYERS, SWA_HEADS * SWA_HEAD_DIM, D_MODEL), (SWA_HEADS * SWA_HEAD_DIM) ** -0.5),
        'swa_b_o': nrm(ks[14], (N_SWA_LAYERS, D_MODEL), 0.02),
        'mla_w_dqkv': nrm(ks[15], (N_MLA_LAYERS, D_MODEL, dqkv_width), D_MODEL ** -0.5),
        'mla_norm_q': gain(ks[16], (N_MLA_LAYERS, MLA_Q_LORA)),
        'mla_norm_kv': gain(ks[17], (N_MLA_LAYERS, MLA_KV_LORA)),
        'mla_w_uq': nrm(ks[18], (N_MLA_LAYERS, MLA_Q_LORA, MLA_HEADS * (MLA_NOPE + MLA_ROPE)), MLA_Q_LORA ** -0.5),
        'mla_w_uk': nrm(ks[19], (N_MLA_LAYERS, MLA_KV_LORA, MLA_HEADS, MLA_NOPE), MLA_KV_LORA ** -0.5),
        'mla_w_uv': nrm(ks[20], (N_MLA_LAYERS, MLA_KV_LORA, MLA_HEADS, MLA_V), MLA_KV_LORA ** -0.5),
        'mla_w_o': nrm(ks[21], (N_MLA_LAYERS, MLA_HEADS * MLA_V, D_MODEL), (MLA_HEADS * MLA_V) ** -0.5),
        'moe_w_group': nrm(ks[22], (DEPTH, D_MODEL, N_GROUPS), D_MODEL ** -0.5),
        'moe_b_group': nrm(ks[23], (DEPTH, N_GROUPS), 0.01),
        'moe_w_expert': nrm(ks[24], (DEPTH, D_MODEL, N_EXPERTS), D_MODEL ** -0.5),
        'moe_b_expert': nrm(ks[25], (DEPTH, N_EXPERTS), 0.01),
        'moe_w_gate': nrm(ks[26], (DEPTH, N_EXPERTS, D_MODEL, D_EXPERT), D_MODEL ** -0.5),
        'moe_w_up': nrm(ks[27], (DEPTH, N_EXPERTS, D_MODEL, D_EXPERT), D_MODEL ** -0.5),
        'moe_w_down': nrm(ks[28], (DEPTH, N_EXPERTS, D_EXPERT, D_MODEL), D_EXPERT ** -0.5),
    }


def reference(x_prompt, x_sample, cache_swa_k, cache_swa_v, cache_mla_latent, cache_mla_krope, page_table,
              norm_attn, norm_ffn, norm_final,
              swa_w_qkv, swa_b_qkv, swa_sinks, swa_w_o, swa_b_o,
              mla_w_dqkv, mla_norm_q, mla_norm_kv, mla_w_uq, mla_w_uk, mla_w_uv, mla_w_o,
              moe_w_group, moe_b_group, moe_w_expert, moe_b_expert, moe_w_gate, moe_w_up, moe_w_down):
    xp, xs = x_prompt, x_sample
    swa_kp, swa_vp, swa_ks, swa_vs = [], [], [], []
    mla_cp, mla_rp, mla_cs, mla_rs = [], [], [], []
    for i in range(DEPTH):
        j = i // N_MIXERS
        hp = rms_norm(xp, norm_attn[i])
        hs = rms_norm(xs, norm_attn[i])
        if i % N_MIXERS == 0:
            ap, kp, vp = swa_prompt(hp, swa_w_qkv[j], swa_b_qkv[j], swa_sinks[j], swa_w_o[j], swa_b_o[j])
            a_s, k_s, v_s = swa_sample(hs, cache_swa_k[j], cache_swa_v[j], swa_w_qkv[j], swa_b_qkv[j],
                                       swa_sinks[j], swa_w_o[j], swa_b_o[j])
            swa_kp.append(kp)
            swa_vp.append(vp)
            swa_ks.append(k_s)
            swa_vs.append(v_s)
        else:
            ap, cp, rp = mla_prompt(hp, mla_w_dqkv[j], mla_norm_q[j], mla_norm_kv[j], mla_w_uq[j],
                                    mla_w_uk[j], mla_w_uv[j], mla_w_o[j])
            a_s, c_s, r_s = mla_sample(hs, cache_mla_latent[j], cache_mla_krope[j], page_table,
                                       mla_w_dqkv[j], mla_norm_q[j], mla_norm_kv[j], mla_w_uq[j],
                                       mla_w_uk[j], mla_w_uv[j], mla_w_o[j])
            mla_cp.append(cp)
            mla_rp.append(rp)
            mla_cs.append(c_s)
            mla_rs.append(r_s)
        xp = xp + ap
        xs = xs + a_s
        fp = rms_norm(xp, norm_ffn[i])
        fs = rms_norm(xs, norm_ffn[i])
        n_p = fp.shape[0] * fp.shape[1]
        tokens = jnp.concatenate([fp.reshape(-1, D_MODEL), fs.reshape(-1, D_MODEL)], axis=0)
        f = hier_moe(tokens, moe_w_group[i], moe_b_group[i], moe_w_expert[i], moe_b_expert[i],
                     moe_w_gate[i], moe_w_up[i], moe_w_down[i])
        xp = xp + f[:n_p].reshape(xp.shape)
        xs = xs + f[n_p:].reshape(xs.shape)
    y_prompt = rms_norm(xp, norm_final)
    y_sample = rms_norm(xs, norm_final)
    return (y_prompt, y_sample,
            jnp.stack(swa_kp), jnp.stack(swa_vp), jnp.stack(swa_ks), jnp.stack(swa_vs),
            jnp.stack(mla_cp), jnp.stack(mla_rp), jnp.stack(mla_cs), jnp.stack(mla_rs))
```

```python
import functools

import jax
import jax.numpy as jnp
from jax import lax
from jax.experimental import pallas as pl
from jax.experimental.pallas import tpu as pltpu

F32 = jnp.float32
BF16 = jnp.bfloat16
I32 = jnp.int32

SWA_HEADS = 16
SWA_KV = 4
SWA_G = SWA_HEADS // SWA_KV
SWA_HD = 64
WINDOW = 128
MLA_HEADS = 16
MLA_Q_LORA = 384
MLA_KV_LORA = 256
MLA_NOPE = 64
MLA_ROPE = 32
MLA_V = 64
ROPE_THETA = 10000.0
N_GROUPS = 8
EXPERTS_PER_GROUP = 8
N_EXPERTS = N_GROUPS * EXPERTS_PER_GROUP
TOP_K = 2
RMS_EPS = 1e-6
NEG_INF = -1e30

LANES = 128
VMEM_LIMIT_BYTES = 52 * 1024 * 1024

MOE_BLOCK = 256
COMBINE_TILE = 256
MLA_TQ = 512
MLA_PAGES_PER_STEP = 8
SWA_SAMPLE_SEQS = 8


def _tile(n, pref):
    t = pref
    while t > 8 and n % t:
        t //= 2
    assert n % t == 0, (n, pref)
    return t


def _params(sem, vmem=VMEM_LIMIT_BYTES):
    return pltpu.CompilerParams(dimension_semantics=sem, vmem_limit_bytes=vmem)


def _log2(n):
    assert n > 0 and n & (n - 1) == 0, n
    return n.bit_length() - 1


def _idiv(v, n):
    return lax.shift_right_logical(v, _log2(n))


def _imod(v, n):
    assert n & (n - 1) == 0, n
    return jnp.bitwise_and(v, n - 1)


def _rms(x, g):
    return x * lax.rsqrt(jnp.mean(x * x, axis=-1, keepdims=True) + RMS_EPS) * g


def _dot(a, b):
    return jnp.dot(a, b, preferred_element_type=F32)


def _dot_nt(a, b):
    return lax.dot_general(a, b, (((1,), (1,)), ((), ())), preferred_element_type=F32)


def _norm_proj_kernel(x_ref, g_ref, w_ref, b_ref, o_ref, wbf_ref):
    @pl.when(pl.program_id(0) == 0)
    def _():
        wbf_ref[...] = w_ref[...].astype(BF16)

    h = _rms(x_ref[...], g_ref[...])
    o_ref[...] = _dot(h.astype(BF16), wbf_ref[...]) + b_ref[...]


def norm_proj(x, g, w, b):
    t, d = x.shape
    n = w.shape[1]
    tm = _tile(t, 512)
    return pl.pallas_call(
        _norm_proj_kernel,
        out_shape=jax.ShapeDtypeStruct((t, n), F32),
        grid=(t // tm,),
        in_specs=[
            pl.BlockSpec((tm, d), lambda i: (i, 0)),
            pl.BlockSpec((1, d), lambda i: (0, 0)),
            pl.BlockSpec((d, n), lambda i: (0, 0)),
            pl.BlockSpec((1, n), lambda i: (0, 0)),
        ],
        out_specs=pl.BlockSpec((tm, n), lambda i: (i, 0)),
        scratch_shapes=[pltpu.VMEM((d, n), BF16)],
        compiler_params=_params(("arbitrary",)),
        name="norm_proj",
    )(x, g.reshape(1, d), w, b.reshape(1, n))


def _block_diag(x, nseg):
    w = x.shape[1] // nseg
    seg = _idiv(lax.broadcasted_iota(I32, x.shape, 1), w)
    zero = jnp.zeros_like(x)
    return jnp.concatenate([jnp.where(seg == s, x, zero) for s in range(nseg)], axis=0)


def _slope(kv, g):
    return 2.0 ** (-8.0 * (kv * SWA_G + g + 1) / SWA_HEADS)


def _swa_prompt_kernel(sink_ref, q_ref, kc_ref, kp_ref, vc_ref, vp_ref, o_ref):
    n = pl.program_id(1)
    kvw = SWA_KV * SWA_HD
    scale = SWA_HD ** -0.5
    k2 = jnp.concatenate([kp_ref[...], kc_ref[...]], axis=0).astype(BF16)
    v2 = jnp.concatenate([vp_ref[...], vc_ref[...]], axis=0).astype(BF16)
    kbd = _block_diag(k2, SWA_KV)
    vbd = _block_diag(v2, SWA_KV)
    qi = lax.broadcasted_iota(I32, (WINDOW, 2 * WINDOW), 0)
    kj = lax.broadcasted_iota(I32, (WINDOW, 2 * WINDOW), 1)
    dist = qi + WINDOW - kj
    first = jnp.where(n > 0, 0, WINDOW)
    valid = (dist >= 0) & (dist < WINDOW) & (kj >= first)
    distf = dist.astype(F32)
    for g in range(SWA_G):
        qg = q_ref[:, g * kvw:(g + 1) * kvw].astype(BF16)
        s = _dot_nt(qg, kbd)
        ps = []
        for kv in range(SWA_KV):
            sink = sink_ref[kv * SWA_G + g]
            sc = s[:, kv * 2 * WINDOW:(kv + 1) * 2 * WINDOW] * scale - _slope(kv, g) * distf
            sc = jnp.where(valid, sc, NEG_INF)
            m = jnp.maximum(jnp.max(sc, axis=-1, keepdims=True), sink)
            p = jnp.exp(sc - m)
            den = jnp.sum(p, axis=-1, keepdims=True) + jnp.exp(sink - m)
            ps.append((p / den).astype(BF16))
        og = _dot(jnp.concatenate(ps, axis=1), vbd)
        o_ref[:, g * kvw:(g + 1) * kvw] = og.astype(BF16)


def swa_prompt_attention(qkv, sinks, batch, seq):
    nb = seq // WINDOW
    nq = SWA_HEADS * SWA_HD
    kvw = SWA_KV * SWA_HD
    kcol = nq // kvw
    cur = lambda b, n: (b * nb + n, kcol)
    prev = lambda b, n: (b * nb + jnp.maximum(n - 1, 0), kcol)
    cur_v = lambda b, n: (b * nb + n, kcol + 1)
    prev_v = lambda b, n: (b * nb + jnp.maximum(n - 1, 0), kcol + 1)
    return pl.pallas_call(
        _swa_prompt_kernel,
        out_shape=jax.ShapeDtypeStruct((batch * seq, nq), BF16),
        grid=(batch, nb),
        in_specs=[
            pl.BlockSpec(memory_space=pltpu.SMEM),
            pl.BlockSpec((WINDOW, nq), lambda b, n: (b * nb + n, 0)),
            pl.BlockSpec((WINDOW, kvw), cur),
            pl.BlockSpec((WINDOW, kvw), prev),
            pl.BlockSpec((WINDOW, kvw), cur_v),
            pl.BlockSpec((WINDOW, kvw), prev_v),
        ],
        out_specs=pl.BlockSpec((WINDOW, nq), lambda b, n: (b * nb + n, 0)),
        compiler_params=_params(("parallel", "parallel")),
        name="swa_prompt",
    )(sinks, qkv, qkv, qkv, qkv, qkv)


def _swa_sample_kernel(dec_t, sink_ref, q_ref, kc_ref, vc_ref, kn_ref, vn_ref, o_ref):
    nrow = SWA_G * dec_t
    scale = SWA_HD ** -0.5
    r = lax.broadcasted_iota(I32, (nrow, 2 * WINDOW), 0)
    j = lax.broadcasted_iota(I32, (nrow, 2 * WINDOW), 1)
    tq = _imod(r, dec_t)
    gq = _idiv(r, dec_t)
    dist = WINDOW + tq - j
    valid = (dist >= 0) & (dist < WINDOW)
    distf = dist.astype(F32)
    g1 = _idiv(lax.broadcasted_iota(I32, (nrow, 1), 0), dec_t)
    npad = 2 * WINDOW - WINDOW - kn_ref.shape[1]
    zpad = jnp.zeros((npad, SWA_KV * SWA_HD), F32)
    for s in range(q_ref.shape[0]):
        kall = jnp.concatenate([kc_ref[s], kn_ref[s], zpad], axis=0).astype(BF16)
        vall = jnp.concatenate([vc_ref[s], vn_ref[s], zpad], axis=0).astype(BF16)
        kbd = _block_diag(kall, SWA_KV)
        vbd = _block_diag(vall, SWA_KV)
        sc_all = _dot_nt(q_ref[s].astype(BF16), kbd)
        ps = []
        for kv in range(SWA_KV):
            slope = jnp.zeros((nrow, 2 * WINDOW), F32)
            sink = jnp.zeros((nrow, 1), F32)
            for g in range(SWA_G):
                slope = jnp.where(gq == g, _slope(kv, g), slope)
                sink = jnp.where(g1 == g, sink_ref[kv * SWA_G + g], sink)
            sc = sc_all[:, kv * 2 * WINDOW:(kv + 1) * 2 * WINDOW] * scale - slope * distf
            sc = jnp.where(valid, sc, NEG_INF)
            m = jnp.maximum(jnp.max(sc, axis=-1, keepdims=True), sink)
            p = jnp.exp(sc - m)
            den = jnp.sum(p, axis=-1, keepdims=True) + jnp.exp(sink - m)
            ps.append((p / den).astype(BF16))
        o_ref[s] = _dot(jnp.concatenate(ps, axis=1), vbd).astype(BF16)


def swa_sample_attention(q_s, k_new, v_new, cache_k, cache_v, sinks, dec_t):
    nbatch, nrow, kvw = q_s.shape
    sb = _tile(nbatch, SWA_SAMPLE_SEQS)
    blk3 = lambda r: pl.BlockSpec((sb, r, kvw), lambda i: (i, 0, 0))
    return pl.pallas_call(
        functools.partial(_swa_sample_kernel, dec_t),
        out_shape=jax.ShapeDtypeStruct((nbatch, nrow, kvw), BF16),
        grid=(nbatch // sb,),
        in_specs=[
            pl.BlockSpec(memory_space=pltpu.SMEM),
            blk3(nrow), blk3(WINDOW), blk3(WINDOW), blk3(k_new.shape[1]), blk3(v_new.shape[1]),
        ],
        out_specs=blk3(nrow),
        compiler_params=_params(("parallel",)),
        name="swa_sample",
    )(sinks, q_s, cache_k, cache_v, k_new, v_new)


def _proj_res_kernel(n_prompt_tiles, x_ref, op_ref, os_ref, w_ref, b_ref, xo_ref, wbf_ref):
    i = pl.program_id(0)

    @pl.when(i == 0)
    def _():
        wbf_ref[...] = w_ref[...].astype(BF16)

    o = jnp.where(i < n_prompt_tiles, op_ref[...], os_ref[...])
    xo_ref[...] = x_ref[...] + (_dot(o, wbf_ref[...]) + b_ref[...])


def proj_residual(x, o_prompt, o_sample, w, b):
    t, d = x.shape
    k = w.shape[0]
    tm = _tile(o_sample.shape[0], 512)
    assert o_prompt.shape[0] % tm == 0
    npt = o_prompt.shape[0] // tm
    return pl.pallas_call(
        functools.partial(_proj_res_kernel, npt),
        out_shape=jax.ShapeDtypeStruct((t, d), F32),
        grid=(t // tm,),
        in_specs=[
            pl.BlockSpec((tm, d), lambda i: (i, 0)),
            pl.BlockSpec((tm, k), lambda i: (jnp.minimum(i, npt - 1), 0)),
            pl.BlockSpec((tm, k), lambda i: (jnp.maximum(i - npt, 0), 0)),
            pl.BlockSpec((k, d), lambda i: (0, 0)),
            pl.BlockSpec((1, d), lambda i: (0, 0)),
        ],
        out_specs=pl.BlockSpec((tm, d), lambda i: (i, 0)),
        scratch_shapes=[pltpu.VMEM((k, d), BF16)],
        compiler_params=_params(("arbitrary",)),
        name="proj_residual",
    )(x, o_prompt, o_sample, w, b.reshape(1, d))


def _split3_dot(a, w_hi, w_lo):
    a_hi = a.astype(BF16)
    a_lo = (a - a_hi.astype(F32)).astype(BF16)
    return _dot(a_hi, w_hi) + (_dot(a_hi, w_lo) + _dot(a_lo, w_hi))


def _router_kernel(x_ref, g_ref, w_ref, b_ref, f_ref, meta_ref, cnt_ref, whi_ref, wlo_ref, carry_ref):
    i = pl.program_id(0)
    tm = x_ref.shape[0]

    @pl.when(i == 0)
    def _():
        w = w_ref[...]
        hi = w.astype(BF16)
        whi_ref[...] = hi
        wlo_ref[...] = (w - hi.astype(F32)).astype(BF16)
        carry_ref[...] = jnp.zeros_like(carry_ref)

    f = _rms(x_ref[...], g_ref[...])
    f_ref[...] = f
    logits = _split3_dot(f, whi_ref[...], wlo_ref[...]) + b_ref[...]
    lane = lax.broadcasted_iota(I32, (tm, LANES), 1)
    lanef = lane.astype(F32)
    big = float(LANES)

    def first_max(mask):
        v = jnp.max(jnp.where(mask, logits, NEG_INF), axis=-1, keepdims=True)
        idx = jnp.min(jnp.where(mask & (logits == v), lanef, big), axis=-1, keepdims=True)
        return v, idx

    gmask = lane < N_GROUPS
    gmax, gidx = first_max(gmask)
    gsum = jnp.sum(jnp.where(gmask, jnp.exp(logits - gmax), 0.0), axis=-1, keepdims=True)
    g_gate = 1.0 / gsum
    lo = N_GROUPS + EXPERTS_PER_GROUP * gidx
    emask = (lanef >= lo) & (lanef < lo + EXPERTS_PER_GROUP)
    e1, i1 = first_max(emask)
    e2, i2 = first_max(emask & (lanef != i1))
    z = jnp.exp(e2 - e1)
    gate1 = g_gate / (1.0 + z)
    gate2 = g_gate * z / (1.0 + z)
    id1 = i1 - N_GROUPS
    id2 = i2 - N_GROUPS
    oh1 = (lanef == id1).astype(F32)
    oh2 = (lanef == id2).astype(F32)
    oh = oh1 + oh2
    rr = lax.broadcasted_iota(I32, (tm, tm), 0)
    cc = lax.broadcasted_iota(I32, (tm, tm), 1)
    tri = (cc < rr).astype(BF16)
    before = _dot(tri, oh.astype(BF16)) + carry_ref[...]
    rank1 = jnp.sum(oh1 * before, axis=-1, keepdims=True)
    rank2 = jnp.sum(oh2 * before, axis=-1, keepdims=True)
    carry_ref[...] = carry_ref[...] + jnp.sum(oh, axis=0, keepdims=True)
    cnt_ref[...] = carry_ref[...]
    meta = jnp.zeros((tm, LANES), F32)
    for k, v in enumerate((id1, id2, gate1, gate2, rank1, rank2)):
        meta = jnp.where(lane == k, v, meta)
    meta_ref[...] = meta


def moe_router(x, g, w_group, b_group, w_expert, b_expert):
    t, d = x.shape
    tm = _tile(t, 512)
    w = jnp.zeros((d, LANES), F32).at[:, :N_GROUPS].set(w_group).at[:, N_GROUPS:N_GROUPS + N_EXPERTS].set(w_expert)
    b = jnp.zeros((1, LANES), F32).at[0, :N_GROUPS].set(b_group).at[0, N_GROUPS:N_GROUPS + N_EXPERTS].set(b_expert)
    return pl.pallas_call(
        _router_kernel,
        out_shape=(
            jax.ShapeDtypeStruct((t, d), F32),
            jax.ShapeDtypeStruct((t, LANES), F32),
            jax.ShapeDtypeStruct((1, LANES), F32),
        ),
        grid=(t // tm,),
        in_specs=[
            pl.BlockSpec((tm, d), lambda i: (i, 0)),
            pl.BlockSpec((1, d), lambda i: (0, 0)),
            pl.BlockSpec((d, LANES), lambda i: (0, 0)),
            pl.BlockSpec((1, LANES), lambda i: (0, 0)),
        ],
        out_specs=(
            pl.BlockSpec((tm, d), lambda i: (i, 0)),
            pl.BlockSpec((tm, LANES), lambda i: (i, 0)),
            pl.BlockSpec((1, LANES), lambda i: (0, 0)),
        ),
        scratch_shapes=[pltpu.VMEM((d, LANES), BF16), pltpu.VMEM((d, LANES), BF16), pltpu.VMEM((1, LANES), F32)],
        compiler_params=_params(("arbitrary",)),
        name="moe_router",
    )(x, g.reshape(1, d), w, b)


def _expert_kernel(blk_e_ref, nblk_ref, src_cur_ref, src_nxt_ref, f_hbm, wg_ref, wu_ref, wd_ref,
                   y_ref, xbuf, sem, wg_bf, wu_bf, wd_bf):
    j = pl.program_id(0)
    nb = nblk_ref[0]
    blk = xbuf.shape[1]

    def row_copy(src_ref, slot, r):
        return pltpu.make_async_copy(f_hbm.at[pl.ds(src_ref[0, 0, r], 1)], xbuf.at[slot, pl.ds(r, 1)],
                                     sem.at[slot])

    def gather(src_ref, slot, wait):
        def body(r, carry):
            cp = row_copy(src_ref, slot, r)
            cp.wait() if wait else cp.start()
            return carry
        lax.fori_loop(0, blk, body, 0, unroll=8)

    @pl.when(j == 0)
    def _():
        gather(src_cur_ref, 0, False)

    @pl.when(j + 1 < nb)
    def _():
        gather(src_nxt_ref, (j + 1) % 2, False)

    @pl.when(j < nb)
    def _():
        slot = j % 2
        gather(src_cur_ref, slot, True)
        e = blk_e_ref[j]
        e_prev = blk_e_ref[jnp.maximum(j - 1, 0)]

        @pl.when(jnp.logical_or(j == 0, e != e_prev))
        def _():
            wg_bf[...] = wg_ref[0].astype(BF16)
            wu_bf[...] = wu_ref[0].astype(BF16)
            wd_bf[...] = wd_ref[0].astype(BF16)

        x = xbuf[slot].astype(BF16)
        a = _dot(x, wg_bf[...])
        u = _dot(x, wu_bf[...])
        h = a * (1.0 / (1.0 + jnp.exp(-a))) * u
        y_ref[...] = _dot(h.astype(BF16), wd_bf[...])

    @pl.when(j >= nb)
    def _():
        y_ref[...] = jnp.zeros_like(y_ref)


def moe_experts(f, src, blk_e, nblk, w_gate, w_up, w_down):
    t, d = f.shape
    nslot = src.shape[0]
    nblocks = nslot // MOE_BLOCK
    de = w_gate.shape[2]
    src3 = src.reshape(nblocks, 1, MOE_BLOCK)
    wmap = lambda j, be, nb: (be[j], 0, 0)
    grid_spec = pltpu.PrefetchScalarGridSpec(
        num_scalar_prefetch=2,
        grid=(nblocks,),
        in_specs=[
            pl.BlockSpec((1, 1, MOE_BLOCK), lambda j, be, nb: (j, 0, 0), memory_space=pltpu.SMEM),
            pl.BlockSpec((1, 1, MOE_BLOCK), lambda j, be, nb: (jnp.minimum(j + 1, nblocks - 1), 0, 0),
                         memory_space=pltpu.SMEM),
            pl.BlockSpec(memory_space=pl.ANY),
            pl.BlockSpec((1, d, de), wmap),
            pl.BlockSpec((1, d, de), wmap),
            pl.BlockSpec((1, de, d), wmap),
        ],
        out_specs=pl.BlockSpec((MOE_BLOCK, d), lambda j, be, nb: (j, 0)),
        scratch_shapes=[
            pltpu.VMEM((2, MOE_BLOCK, d), F32),
            pltpu.SemaphoreType.DMA((2,)),
            pltpu.VMEM((d, de), BF16),
            pltpu.VMEM((d, de), BF16),
            pltpu.VMEM((de, d), BF16),
        ],
    )
    return pl.pallas_call(
        _expert_kernel,
        out_shape=jax.ShapeDtypeStruct((nslot, d), F32),
        grid_spec=grid_spec,
        compiler_params=_params(("arbitrary",)),
        name="moe_experts",
    )(blk_e, nblk, src3, src3, f, w_gate, w_up, w_down)


def _combine_kernel(final_norm, d1c_ref, d2c_ref, d1n_ref, d2n_ref, yb_hbm, x_ref, meta_ref, g_ref,
                    o_ref, ybuf, sem):
    i = pl.program_id(0)
    n = pl.num_programs(0)
    tm = x_ref.shape[0]

    def row_copy(d_ref, slot, k, r):
        return pltpu.make_async_copy(yb_hbm.at[pl.ds(d_ref[0, 0, r], 1)], ybuf.at[slot, k, pl.ds(r, 1)],
                                     sem.at[slot])

    def gather(d1_ref, d2_ref, slot, wait):
        def body(r, carry):
            for k, d_ref in enumerate((d1_ref, d2_ref)):
                cp = row_copy(d_ref, slot, k, r)
                cp.wait() if wait else cp.start()
            return carry
        lax.fori_loop(0, tm, body, 0, unroll=8)

    @pl.when(i == 0)
    def _():
        gather(d1c_ref, d2c_ref, 0, False)

    @pl.when(i + 1 < n)
    def _():
        gather(d1n_ref, d2n_ref, (i + 1) % 2, False)

    slot = i % 2
    gather(d1c_ref, d2c_ref, slot, True)
    meta = meta_ref[...]
    gate1 = meta[:, 2:3]
    gate2 = meta[:, 3:4]
    out = x_ref[...] + (ybuf[slot, 0] * gate1 + ybuf[slot, 1] * gate2)
    if final_norm:
        out = _rms(out, g_ref[...])
    o_ref[...] = out


def moe_combine(x, yb, dest, meta, g_final, final_norm):
    t, d = x.shape
    tm = _tile(t, COMBINE_TILE)
    nt = t // tm
    d1 = dest[:, 0].reshape(nt, 1, tm)
    d2 = dest[:, 1].reshape(nt, 1, tm)
    cur = pl.BlockSpec((1, 1, tm), lambda i: (i, 0, 0), memory_space=pltpu.SMEM)
    nxt = pl.BlockSpec((1, 1, tm), lambda i: (jnp.minimum(i + 1, nt - 1), 0, 0), memory_space=pltpu.SMEM)
    return pl.pallas_call(
        functools.partial(_combine_kernel, final_norm),
        out_shape=jax.ShapeDtypeStruct((t, d), F32),
        grid=(nt,),
        in_specs=[
            cur, cur, nxt, nxt,
            pl.BlockSpec(memory_space=pl.ANY),
            pl.BlockSpec((tm, d), lambda i: (i, 0)),
            pl.BlockSpec((tm, LANES), lambda i: (i, 0)),
            pl.BlockSpec((1, d), lambda i: (0, 0)),
        ],
        out_specs=pl.BlockSpec((tm, d), lambda i: (i, 0)),
        scratch_shapes=[pltpu.VMEM((2, TOP_K, tm, d), F32), pltpu.SemaphoreType.DMA((2,))],
        compiler_params=_params(("arbitrary",)),
        name="moe_combine",
    )(d1, d2, d1, d2, yb, x, meta, g_final.reshape(1, d))


def hier_moe_layer(x, norm_g, w_group, b_group, w_expert, b_expert, w_gate, w_up, w_down, g_final, final_norm):
    t, d = x.shape
    f, meta, counts = moe_router(x, norm_g, w_group, b_group, w_expert, b_expert)
    ids = meta[:, 0:2].astype(I32)
    rank = meta[:, 4:6].astype(I32)
    cnt = counts[0, :N_EXPERTS].astype(I32)
    padded = (cnt + MOE_BLOCK - 1) // MOE_BLOCK * MOE_BLOCK
    pend = jnp.cumsum(padded)
    pstart = pend - padded
    dest = pstart[ids] + rank
    nblocks = -(-(t * TOP_K) // MOE_BLOCK) + N_EXPERTS
    nslot = nblocks * MOE_BLOCK
    tok = jnp.broadcast_to(jnp.arange(t, dtype=I32)[:, None], (t, TOP_K))
    src = jnp.zeros((nslot,), I32).at[dest.reshape(-1)].set(tok.reshape(-1))
    blk_e = jnp.minimum(jnp.searchsorted(pend, jnp.arange(nblocks, dtype=I32) * MOE_BLOCK, side="right"),
                        N_EXPERTS - 1).astype(I32)
    nblk = (pend[-1:] // MOE_BLOCK).astype(I32)
    yb = moe_experts(f, src, blk_e, nblk, w_gate, w_up, w_down)
    return moe_combine(x, yb, dest, meta, g_final, final_norm)


def _mla_down_kernel(x_ref, g_ref, w_ref, gq_ref, gkv_ref, cos_ref, sin_ref, cq_ref, ckr_ref, wbf_ref):
    @pl.when(pl.program_id(0) == 0)
    def _():
        wbf_ref[...] = w_ref[...].astype(BF16)

    h = _rms(x_ref[...], g_ref[...])
    a = _dot(h.astype(BF16), wbf_ref[...])
    q0, c0 = MLA_Q_LORA, MLA_Q_LORA + MLA_KV_LORA
    cq_ref[...] = _rms(a[:, :q0], gq_ref[...]).astype(BF16)
    ckr_ref[:, :MLA_KV_LORA] = _rms(a[:, q0:c0], gkv_ref[...])
    ckr_ref[:, MLA_KV_LORA:] = a[:, c0:c0 + LANES] * cos_ref[...] + a[:, c0 + LANES:] * sin_ref[...]


def _rot_cols(w):
    half = MLA_ROPE // 2
    return jnp.concatenate([-w[..., half:], w[..., :half]], axis=-1)


def mla_down(x, g, w_dqkv, norm_q, norm_kv, cos_t, sin_t):
    t, d = x.shape
    tm = _tile(t, 512)
    q0, c0 = MLA_Q_LORA, MLA_Q_LORA + MLA_KV_LORA
    w_r = w_dqkv[:, c0:]
    zpad = jnp.zeros((d, LANES - MLA_ROPE), F32)
    w = jnp.concatenate([w_dqkv[:, :c0], w_r, zpad, _rot_cols(w_r), zpad], axis=1)
    n = w.shape[1]
    return pl.pallas_call(
        _mla_down_kernel,
        out_shape=(jax.ShapeDtypeStruct((t, q0), BF16), jax.ShapeDtypeStruct((t, MLA_KV_LORA + LANES), F32)),
        grid=(t // tm,),
        in_specs=[
            pl.BlockSpec((tm, d), lambda i: (i, 0)),
            pl.BlockSpec((1, d), lambda i: (0, 0)),
            pl.BlockSpec((d, n), lambda i: (0, 0)),
            pl.BlockSpec((1, q0), lambda i: (0, 0)),
            pl.BlockSpec((1, MLA_KV_LORA), lambda i: (0, 0)),
            pl.BlockSpec((tm, LANES), lambda i: (i, 0)),
            pl.BlockSpec((tm, LANES), lambda i: (i, 0)),
        ],
        out_specs=(pl.BlockSpec((tm, q0), lambda i: (i, 0)),
                   pl.BlockSpec((tm, MLA_KV_LORA + LANES), lambda i: (i, 0))),
        scratch_shapes=[pltpu.VMEM((d, n), BF16)],
        compiler_params=_params(("arbitrary",)),
        name="mla_down",
    )(x, g.reshape(1, d), w, norm_q.reshape(1, q0), norm_kv.reshape(1, MLA_KV_LORA), cos_t, sin_t)


def _mla_up_kernel(cq_ref, ckr_ref, qc_ref, qs_ref, wq_ref, wk_ref, wv_ref, q_ref, k_ref, v_ref,
                   wq_bf, wk_bf, wv_bf):
    @pl.when(pl.program_id(0) == 0)
    def _():
        wq_bf[...] = wq_ref[...].astype(BF16)
        wk_bf[...] = wk_ref[...].astype(BF16)
        wv_bf[...] = wv_ref[...].astype(BF16)

    nq = q_ref.shape[1]
    ab = _dot(cq_ref[...], wq_bf[...])
    qc = qc_ref[...]
    qs = qs_ref[...]
    for h in range(MLA_HEADS):
        sl = slice(h * LANES, (h + 1) * LANES)
        sl_b = slice(nq + h * LANES, nq + (h + 1) * LANES)
        q_ref[:, sl] = (ab[:, sl] * qc + ab[:, sl_b] * qs).astype(BF16)
    ckr = ckr_ref[...].astype(BF16)
    k_ref[...] = _dot(ckr, wk_bf[...]).astype(BF16)
    v_ref[...] = _dot(ckr[:, :MLA_KV_LORA], wv_bf[...]).astype(BF16)


def _head_slots(w, lo):
    r, h, n = w.shape
    out = jnp.zeros((r, h, LANES), F32).at[:, :, lo:lo + n].set(w)
    return out.reshape(r, h * LANES)


def mla_up(cq, ckr, q_cos, q_sin, w_uq, w_uk, w_uv):
    t = cq.shape[0]
    tm = _tile(t, 256)
    nq = MLA_HEADS * LANES
    kin = MLA_KV_LORA + LANES
    wq3 = w_uq.reshape(MLA_Q_LORA, MLA_HEADS, MLA_NOPE + MLA_ROPE)
    wq_a = _head_slots(wq3, 0)
    wq_b = _head_slots(_rot_cols(wq3[:, :, MLA_NOPE:]), MLA_NOPE)
    wq = jnp.concatenate([wq_a, wq_b], axis=1)
    place = jnp.zeros((LANES, MLA_HEADS, LANES), F32)
    place = place.at[jnp.arange(MLA_ROPE), :, MLA_NOPE + jnp.arange(MLA_ROPE)].set(1.0)
    wk = jnp.concatenate([_head_slots(w_uk, 0), place.reshape(LANES, nq)], axis=0)
    even = (jnp.arange(MLA_HEADS) % 2 == 0)[None, :, None]
    wv = jnp.where(even, _head_slots(w_uv, 0).reshape(MLA_KV_LORA, MLA_HEADS, LANES),
                   _head_slots(w_uv, MLA_V).reshape(MLA_KV_LORA, MLA_HEADS, LANES)).reshape(MLA_KV_LORA, nq)
    full = lambda r, c: pl.BlockSpec((r, c), lambda i: (0, 0))
    rows = lambda c: pl.BlockSpec((tm, c), lambda i: (i, 0))
    return pl.pallas_call(
        _mla_up_kernel,
        out_shape=tuple(jax.ShapeDtypeStruct((t, nq), BF16) for _ in range(3)),
        grid=(t // tm,),
        in_specs=[rows(MLA_Q_LORA), rows(kin), rows(LANES), rows(LANES),
                  full(MLA_Q_LORA, 2 * nq), full(kin, nq), full(MLA_KV_LORA, nq)],
        out_specs=(rows(nq), rows(nq), rows(nq)),
        scratch_shapes=[pltpu.VMEM((MLA_Q_LORA, 2 * nq), BF16), pltpu.VMEM((kin, nq), BF16),
                        pltpu.VMEM((MLA_KV_LORA, nq), BF16)],
        compiler_params=_params(("arbitrary",)),
        name="mla_up",
    )(cq, ckr, q_cos, q_sin, wq, wk, wv)


def _mla_flash_kernel(q_ref, k_ref, v_ref, o_ref, m_ref, l_ref, acc_ref):
    qi = pl.program_id(2)
    tq = q_ref.shape[0]
    lane = lax.broadcasted_iota(I32, (tq, LANES), 1)
    first_half = lane < MLA_V
    m_ref[...] = jnp.full_like(m_ref, NEG_INF)
    l_ref[...] = jnp.zeros_like(l_ref)
    acc_ref[...] = jnp.zeros_like(acc_ref)
    q = q_ref[...]

    def step(kb, masked):
        start = pl.multiple_of(kb * tq, tq)
        kblk = k_ref[pl.ds(start, tq), :]
        vblk = v_ref[pl.ds(start, tq), :]
        pv = None
        alphas = []
        for h in range(2):
            s = _dot_nt(q[:, h * LANES:(h + 1) * LANES], kblk[:, h * LANES:(h + 1) * LANES])
            if masked:
                row = lax.broadcasted_iota(I32, s.shape, 0)
                col = lax.broadcasted_iota(I32, s.shape, 1)
                s = jnp.where(col <= row, s, NEG_INF)
            m_old = m_ref[h]
            m_new = jnp.maximum(m_old, jnp.max(s, axis=-1, keepdims=True))
            alpha = jnp.exp(m_old - m_new)
            p = jnp.exp(s - m_new)
            l_ref[h] = alpha * l_ref[h] + jnp.sum(p, axis=-1, keepdims=True)
            m_ref[h] = m_new
            part = _dot(p.astype(BF16), vblk[:, h * LANES:(h + 1) * LANES])
            pv = part if pv is None else pv + part
            alphas.append(alpha)
        acc_ref[...] = jnp.where(first_half, alphas[0], alphas[1]) * acc_ref[...] + pv

    def body(kb, carry):
        step(kb, False)
        return carry

    lax.fori_loop(0, qi, body, 0)
    step(qi, True)
    o_ref[...] = (acc_ref[...] * jnp.where(first_half, 1.0 / l_ref[0], 1.0 / l_ref[1])).astype(BF16)


def mla_prompt_attention(qp, kp, vp, batch, seq):
    tq = _tile(seq, MLA_TQ)
    nqb = seq // tq
    npair = MLA_HEADS // 2
    pw = 2 * LANES
    return pl.pallas_call(
        _mla_flash_kernel,
        out_shape=jax.ShapeDtypeStruct((batch * seq, MLA_HEADS * MLA_V), BF16),
        grid=(batch, npair, nqb),
        in_specs=[
            pl.BlockSpec((tq, pw), lambda b, h, i: (b * nqb + i, h)),
            pl.BlockSpec((seq, pw), lambda b, h, i: (b, h)),
            pl.BlockSpec((seq, pw), lambda b, h, i: (b, h)),
        ],
        out_specs=pl.BlockSpec((tq, LANES), lambda b, h, i: (b * nqb + i, h)),
        scratch_shapes=[pltpu.VMEM((2, tq, 1), F32), pltpu.VMEM((2, tq, 1), F32), pltpu.VMEM((tq, LANES), F32)],
        compiler_params=_params(("parallel", "parallel", "arbitrary")),
        name="mla_prompt",
    )(qp, kp, vp)


def _mla_absorb_kernel(q_ref, w_ref, o_ref):
    o_ref[0] = _dot(q_ref[...], w_ref[0].astype(BF16)).astype(BF16)


def mla_absorb(q_s, w_uk):
    ns = q_s.shape[0]
    kin = MLA_KV_LORA + LANES
    w = jnp.zeros((MLA_HEADS, LANES, kin), F32)
    w = w.at[:, :MLA_NOPE, :MLA_KV_LORA].set(jnp.transpose(w_uk, (1, 2, 0)))
    w = w.at[:, MLA_NOPE + jnp.arange(MLA_ROPE), MLA_KV_LORA + jnp.arange(MLA_ROPE)].set(1.0)
    return pl.pallas_call(
        _mla_absorb_kernel,
        out_shape=jax.ShapeDtypeStruct((MLA_HEADS, ns, kin), BF16),
        grid=(MLA_HEADS,),
        in_specs=[pl.BlockSpec((ns, LANES), lambda h: (0, h)),
                  pl.BlockSpec((1, LANES, kin), lambda h: (h, 0, 0))],
        out_specs=pl.BlockSpec((1, ns, kin), lambda h: (h, 0, 0)),
        compiler_params=_params(("parallel",)),
        name="mla_absorb",
    )(q_s, w)


def _mla_decode_kernel(npg, dec_t, pt_ref, q_ref, new_ref, *refs):
    lat_refs = refs[:npg]
    kr_refs = refs[npg:2 * npg]
    o_ref, m_ref, l_ref, acc_ref = refs[2 * npg:]
    c = pl.program_id(1)
    nrow = q_ref.shape[0]

    @pl.when(c == 0)
    def _():
        m_ref[...] = jnp.full_like(m_ref, NEG_INF)
        l_ref[...] = jnp.zeros_like(l_ref)
        acc_ref[...] = jnp.zeros_like(acc_ref)

    q = q_ref[...]
    ql = q[:, :MLA_KV_LORA]
    qr = q[:, MLA_KV_LORA:MLA_KV_LORA + MLA_ROPE]
    lat = jnp.concatenate([r[...] for r in lat_refs], axis=0).astype(BF16)
    kr = jnp.concatenate([r[...] for r in kr_refs], axis=0).astype(BF16)
    s = _dot_nt(ql, lat) + _dot_nt(qr, kr)
    m_old = m_ref[...]
    m_new = jnp.maximum(m_old, jnp.max(s, axis=-1, keepdims=True))
    alpha = jnp.exp(m_old - m_new)
    p = jnp.exp(s - m_new)
    l_ref[...] = alpha * l_ref[...] + jnp.sum(p, axis=-1, keepdims=True)
    acc_ref[...] = alpha * acc_ref[...] + _dot(p.astype(BF16), lat)
    m_ref[...] = m_new

    @pl.when(c == pl.num_programs(1) - 1)
    def _():
        new = new_ref[...].astype(BF16)
        sn = _dot_nt(q, new)
        t_row = _idiv(lax.broadcasted_iota(I32, sn.shape, 0), MLA_HEADS)
        t_col = lax.broadcasted_iota(I32, sn.shape, 1)
        sn = jnp.where(t_col <= t_row, sn, NEG_INF)
        m1 = m_ref[...]
        m2 = jnp.maximum(m1, jnp.max(sn, axis=-1, keepdims=True))
        a2 = jnp.exp(m1 - m2)
        pn = jnp.exp(sn - m2)
        l2 = a2 * l_ref[...] + jnp.sum(pn, axis=-1, keepdims=True)
        pnb = pn.astype(BF16).astype(F32)
        cn = new[:, :MLA_KV_LORA].astype(F32)
        acc = a2 * acc_ref[...]
        for t in range(dec_t):
            acc = acc + pnb[:, t:t + 1] * cn[t:t + 1, :]
        o_ref[...] = (acc / l2).astype(BF16)


def mla_decode(qabs, ckr_new, cache_lat, cache_kr, page_table, dec_t):
    nbatch, npages = page_table.shape
    npg = _tile(npages, MLA_PAGES_PER_STEP)
    nrow = dec_t * MLA_HEADS
    kin = MLA_KV_LORA + LANES
    page = cache_lat.shape[1]

    def page_map(i):
        return lambda b, c, pt: (pt[b, c * npg + i], 0, 0)

    grid_spec = pltpu.PrefetchScalarGridSpec(
        num_scalar_prefetch=1,
        grid=(nbatch, npages // npg),
        in_specs=[pl.BlockSpec((nrow, kin), lambda b, c, pt: (b, 0)),
                  pl.BlockSpec((None, 8, kin), lambda b, c, pt: (b, 0, 0))]
        + [pl.BlockSpec((None, page, MLA_KV_LORA), page_map(i)) for i in range(npg)]
        + [pl.BlockSpec((None, page, MLA_ROPE), page_map(i)) for i in range(npg)],
        out_specs=pl.BlockSpec((nrow, MLA_KV_LORA), lambda b, c, pt: (b, 0)),
        scratch_shapes=[pltpu.VMEM((nrow, 1), F32), pltpu.VMEM((nrow, 1), F32),
                        pltpu.VMEM((nrow, MLA_KV_LORA), F32)],
    )
    return pl.pallas_call(
        functools.partial(_mla_decode_kernel, npg, dec_t),
        out_shape=jax.ShapeDtypeStruct((nbatch * nrow, MLA_KV_LORA), BF16),
        grid_spec=grid_spec,
        compiler_params=_params(("parallel", "arbitrary")),
        name="mla_decode",
    )(page_table, qabs, ckr_new, *([cache_lat] * npg), *([cache_kr] * npg))


def _mla_unabsorb_kernel(ol_ref, w_ref, o_ref):
    r = _dot(ol_ref[...], w_ref[...].astype(BF16))
    row_h = _imod(lax.broadcasted_iota(I32, r.shape, 0), MLA_HEADS)
    col_h = _idiv(lax.broadcasted_iota(I32, r.shape, 1), MLA_V)
    r = jnp.where(row_h == col_h, r, 0.0)
    o_ref[...] = jnp.sum(r.reshape(r.shape[0] // MLA_HEADS, MLA_HEADS, r.shape[1]), axis=1).astype(BF16)


def mla_unabsorb(o_lat, w_uv):
    rows = o_lat.shape[0]
    tr = _tile(rows, 1024)
    n = MLA_HEADS * MLA_V
    return pl.pallas_call(
        _mla_unabsorb_kernel,
        out_shape=jax.ShapeDtypeStruct((rows // MLA_HEADS, n), BF16),
        grid=(rows // tr,),
        in_specs=[pl.BlockSpec((tr, MLA_KV_LORA), lambda i: (i, 0)),
                  pl.BlockSpec((MLA_KV_LORA, n), lambda i: (0, 0))],
        out_specs=pl.BlockSpec((tr // MLA_HEADS, n), lambda i: (i, 0)),
        compiler_params=_params(("parallel",)),
        name="mla_unabsorb",
    )(o_lat, w_uv.reshape(MLA_KV_LORA, n))


def _rope_tables(pos):
    half = MLA_ROPE // 2
    inv = jnp.power(ROPE_THETA, -jnp.arange(half, dtype=F32) * 2.0 / MLA_ROPE)
    ang = pos[:, None] * inv[None, :]
    cos2 = jnp.tile(jnp.cos(ang), (1, 2))
    sin2 = jnp.tile(jnp.sin(ang), (1, 2))
    t = pos.shape[0]
    scale = (MLA_NOPE + MLA_ROPE) ** -0.5
    k_cos = jnp.zeros((t, LANES), F32).at[:, :MLA_ROPE].set(cos2)
    k_sin = jnp.zeros((t, LANES), F32).at[:, :MLA_ROPE].set(sin2)
    q_cos = jnp.zeros((t, LANES), F32).at[:, :MLA_NOPE].set(scale).at[:, MLA_NOPE:MLA_NOPE + MLA_ROPE].set(scale * cos2)
    q_sin = jnp.zeros((t, LANES), F32).at[:, MLA_NOPE:MLA_NOPE + MLA_ROPE].set(scale * sin2)
    return k_cos, k_sin, q_cos, q_sin


def kernel(x_prompt, x_sample, cache_swa_k, cache_swa_v, cache_mla_latent, cache_mla_krope, page_table,
           norm_attn, norm_ffn, norm_final,
           swa_w_qkv, swa_b_qkv, swa_sinks, swa_w_o, swa_b_o,
           mla_w_dqkv, mla_norm_q, mla_norm_kv, mla_w_uq, mla_w_uk, mla_w_uv, mla_w_o,
           moe_w_group, moe_b_group, moe_w_expert, moe_b_expert, moe_w_gate, moe_w_up, moe_w_down):
    batch, seq, d = x_prompt.shape
    nbatch, dec_t, _ = x_sample.shape
    n_p = batch * seq
    n_s = nbatch * dec_t
    npages = page_table.shape[1]
    page = cache_mla_latent.shape[2]
    past_len = npages * page
    nq = SWA_HEADS * SWA_HD
    kvw = SWA_KV * SWA_HD
    x = jnp.concatenate([x_prompt.reshape(n_p, d), x_sample.reshape(n_s, d)], axis=0)

    w_qkv = swa_w_qkv[0]
    w_q = w_qkv[:, :nq].reshape(d, SWA_KV, SWA_G, SWA_HD).transpose(0, 2, 1, 3).reshape(d, nq)
    b_q = swa_b_qkv[0][:nq].reshape(SWA_KV, SWA_G, SWA_HD).transpose(1, 0, 2).reshape(nq)
    w0 = jnp.concatenate([w_q, w_qkv[:, nq:]], axis=1)
    b0 = jnp.concatenate([b_q, swa_b_qkv[0][nq:]])
    w_o0 = swa_w_o[0].reshape(SWA_KV, SWA_G, SWA_HD, d).transpose(1, 0, 2, 3).reshape(nq, d)
    qkv = norm_proj(x, norm_attn[0], w0, b0)
    o_p = swa_prompt_attention(qkv, swa_sinks[0], batch, seq)
    qkv_s = qkv[n_p:]
    q_s = qkv_s[:, :nq].reshape(nbatch, dec_t, SWA_G, kvw).transpose(0, 2, 1, 3).reshape(nbatch, SWA_G * dec_t, kvw)
    k_s = qkv_s[:, nq:nq + kvw].reshape(nbatch, dec_t, kvw)
    v_s = qkv_s[:, nq + kvw:].reshape(nbatch, dec_t, kvw)
    pad8 = lambda a: jnp.pad(a, ((0, 0), (0, 8 - dec_t), (0, 0)))
    ck = cache_swa_k[0].reshape(nbatch, WINDOW, kvw)
    cv = cache_swa_v[0].reshape(nbatch, WINDOW, kvw)
    o_s = swa_sample_attention(q_s, pad8(k_s), pad8(v_s), ck, cv, swa_sinks[0], dec_t)
    o_s = o_s.reshape(nbatch, SWA_G, dec_t, kvw).transpose(0, 2, 1, 3).reshape(n_s, nq)
    x = proj_residual(x, o_p, o_s, w_o0, swa_b_o[0])
    x = hier_moe_layer(x, norm_ffn[0], moe_w_group[0], moe_b_group[0], moe_w_expert[0], moe_b_expert[0],
                       moe_w_gate[0], moe_w_up[0], moe_w_down[0], norm_final, False)

    k_p = qkv[:n_p, nq:nq + kvw].reshape(batch, seq, SWA_KV, SWA_HD)
    v_p = qkv[:n_p, nq + kvw:].reshape(batch, seq, SWA_KV, SWA_HD)
    swa_kp = k_p[:, seq - WINDOW:][None]
    swa_vp = v_p[:, seq - WINDOW:][None]
    swa_ks = jnp.concatenate([cache_swa_k[0], k_s.reshape(nbatch, dec_t, SWA_KV, SWA_HD)], axis=1)[:, dec_t:][None]
    swa_vs = jnp.concatenate([cache_swa_v[0], v_s.reshape(nbatch, dec_t, SWA_KV, SWA_HD)], axis=1)[:, dec_t:][None]

    pos = jnp.concatenate([jnp.tile(jnp.arange(seq, dtype=F32), batch),
                           jnp.tile(past_len + jnp.arange(dec_t, dtype=F32), nbatch)])
    k_cos, k_sin, q_cos, q_sin = _rope_tables(pos)
    cq, ckr = mla_down(x, norm_attn[1], mla_w_dqkv[0], mla_norm_q[0], mla_norm_kv[0], k_cos, k_sin)
    qp, kp, vp = mla_up(cq, ckr, q_cos, q_sin, mla_w_uq[0], mla_w_uk[0], mla_w_uv[0])
    o_p = mla_prompt_attention(qp, kp, vp, batch, seq)
    qabs = mla_absorb(qp[n_p:], mla_w_uk[0])
    qabs = qabs.transpose(1, 0, 2).reshape(n_s * MLA_HEADS, MLA_KV_LORA + LANES)
    ckr_new = jnp.pad(ckr[n_p:].reshape(nbatch, dec_t, MLA_KV_LORA + LANES), ((0, 0), (0, 8 - dec_t), (0, 0)))
    o_lat = mla_decode(qabs, ckr_new, cache_mla_latent[0], cache_mla_krope[0], page_table, dec_t)
    o_s = mla_unabsorb(o_lat, mla_w_uv[0])
    x = proj_residual(x, o_p, o_s, mla_w_o[0], jnp.zeros((d,), F32))
    y = hier_moe_layer(x, norm_ffn[1], moe_w_group[1], moe_b_group[1], moe_w_expert[1], moe_b_expert[1],
                       moe_w_gate[1], moe_w_up[1], moe_w_down[1], norm_final, True)

    c_all = ckr[:, :MLA_KV_LORA]
    r_all = ckr[:, MLA_KV_LORA:MLA_KV_LORA + MLA_ROPE]
    return (y[:n_p].reshape(batch, seq, d), y[n_p:].reshape(nbatch, dec_t, d),
            swa_kp, swa_vp, swa_ks, swa_vs,
            c_all[:n_p].reshape(1, batch, seq, MLA_KV_LORA), r_all[:n_p].reshape(1, batch, seq, MLA_ROPE),
            c_all[n_p:].reshape(1, nbatch, dec_t, MLA_KV_LORA), r_all[n_p:].reshape(1, nbatch, dec_t, MLA_ROPE))
```

```python
import functools

import jax
import jax.numpy as jnp
from jax import lax
from jax.experimental import pallas as pl
from jax.experimental.pallas import tpu as pltpu

F32 = jnp.float32
BF16 = jnp.bfloat16
I32 = jnp.int32

SWA_HEADS = 16
SWA_KV = 4
SWA_G = SWA_HEADS // SWA_KV
SWA_HD = 64
WINDOW = 128
MLA_HEADS = 16
MLA_Q_LORA = 384
MLA_KV_LORA = 256
MLA_NOPE = 64
MLA_ROPE = 32
MLA_V = 64
ROPE_THETA = 10000.0
N_GROUPS = 8
EXPERTS_PER_GROUP = 8
N_EXPERTS = N_GROUPS * EXPERTS_PER_GROUP
TOP_K = 2
RMS_EPS = 1e-6
NEG_INF = -1e30

LANES = 128
VMEM_LIMIT_BYTES = 52 * 1024 * 1024

MOE_BLOCK = 256
COMBINE_TILE = 256
MLA_TQ = 512
MLA_PAGES_PER_STEP = 8
SWA_SAMPLE_SEQS = 8


def _tile(n, pref):
    t = pref
    while t > 8 and n % t:
        t //= 2
    assert n % t == 0, (n, pref)
    return t


def _params(sem, vmem=VMEM_LIMIT_BYTES):
    return pltpu.CompilerParams(dimension_semantics=sem, vmem_limit_bytes=vmem)


def _log2(n):
    assert n > 0 and n & (n - 1) == 0, n
    return n.bit_length() - 1


def _idiv(v, n):
    return lax.shift_right_logical(v, _log2(n))


def _imod(v, n):
    assert n & (n - 1) == 0, n
    return jnp.bitwise_and(v, n - 1)


def _rms(x, g):
    return x * lax.rsqrt(jnp.mean(x * x, axis=-1, keepdims=True) + RMS_EPS) * g


def _dot(a, b):
    return jnp.dot(a, b, preferred_element_type=F32)


def _dot_nt(a, b):
    return lax.dot_general(a, b, (((1,), (1,)), ((), ())), preferred_element_type=F32)


def _norm_proj_kernel(x_ref, g_ref, w_ref, b_ref, o_ref, wbf_ref):
    @pl.when(pl.program_id(0) == 0)
    def _():
        wbf_ref[...] = w_ref[...].astype(BF16)

    h = _rms(x_ref[...], g_ref[...])
    o_ref[...] = _dot(h.astype(BF16), wbf_ref[...]) + b_ref[...]


def norm_proj(x, g, w, b):
    t, d = x.shape
    n = w.shape[1]
    tm = _tile(t, 512)
    return pl.pallas_call(
        _norm_proj_kernel,
        out_shape=jax.ShapeDtypeStruct((t, n), F32),
        grid=(t // tm,),
        in_specs=[
            pl.BlockSpec((tm, d), lambda i: (i, 0)),
            pl.BlockSpec((1, d), lambda i: (0, 0)),
            pl.BlockSpec((d, n), lambda i: (0, 0)),
            pl.BlockSpec((1, n), lambda i: (0, 0)),
        ],
        out_specs=pl.BlockSpec((tm, n), lambda i: (i, 0)),
        scratch_shapes=[pltpu.VMEM((d, n), BF16)],
        compiler_params=_params(("arbitrary",)),
        name="norm_proj",
    )(x, g.reshape(1, d), w, b.reshape(1, n))


def _block_diag(x, nseg):
    w = x.shape[1] // nseg
    seg = _idiv(lax.broadcasted_iota(I32, x.shape, 1), w)
    zero = jnp.zeros_like(x)
    return jnp.concatenate([jnp.where(seg == s, x, zero) for s in range(nseg)], axis=0)


def _slope(kv, g):
    return 2.0 ** (-8.0 * (kv * SWA_G + g + 1) / SWA_HEADS)


def _swa_prompt_kernel(sink_ref, q_ref, kc_ref, kp_ref, vc_ref, vp_ref, o_ref):
    n = pl.program_id(1)
    kvw = SWA_KV * SWA_HD
    scale = SWA_HD ** -0.5
    k2 = jnp.concatenate([kp_ref[...], kc_ref[...]], axis=0).astype(BF16)
    v2 = jnp.concatenate([vp_ref[...], vc_ref[...]], axis=0).astype(BF16)
    kbd = _block_diag(k2, SWA_KV)
    vbd = _block_diag(v2, SWA_KV)
    qi = lax.broadcasted_iota(I32, (WINDOW, 2 * WINDOW), 0)
    kj = lax.broadcasted_iota(I32, (WINDOW, 2 * WINDOW), 1)
    dist = qi + WINDOW - kj
    first = jnp.where(n > 0, 0, WINDOW)
    valid = (dist >= 0) & (dist < WINDOW) & (kj >= first)
    distf = dist.astype(F32)
    for g in range(SWA_G):
        qg = q_ref[:, g * kvw:(g + 1) * kvw].astype(BF16)
        s = _dot_nt(qg, kbd)
        ps = []
        for kv in range(SWA_KV):
            sink = sink_ref[kv * SWA_G + g]
            sc = s[:, kv * 2 * WINDOW:(kv + 1) * 2 * WINDOW] * scale - _slope(kv, g) * distf
            sc = jnp.where(valid, sc, NEG_INF)
            m = jnp.maximum(jnp.max(sc, axis=-1, keepdims=True), sink)
            p = jnp.exp(sc - m)
            den = jnp.sum(p, axis=-1, keepdims=True) + jnp.exp(sink - m)
            ps.append((p / den).astype(BF16))
        og = _dot(jnp.concatenate(ps, axis=1), vbd)
        o_ref[:, g * kvw:(g + 1) * kvw] = og.astype(BF16)


def swa_prompt_attention(qkv, sinks, batch, seq):
    nb = seq // WINDOW
    nq = SWA_HEADS * SWA_HD
    kvw = SWA_KV * SWA_HD
    kcol = nq // kvw
    cur = lambda b, n: (b * nb + n, kcol)
    prev = lambda b, n: (b * nb + jnp.maximum(n - 1, 0), kcol)
    cur_v = lambda b, n: (b * nb + n, kcol + 1)
    prev_v = lambda b, n: (b * nb + jnp.maximum(n - 1, 0), kcol + 1)
    return pl.pallas_call(
        _swa_prompt_kernel,
        out_shape=jax.ShapeDtypeStruct((batch * seq, nq), BF16),
        grid=(batch, nb),
        in_specs=[
            pl.BlockSpec(memory_space=pltpu.SMEM),
            pl.BlockSpec((WINDOW, nq), lambda b, n: (b * nb + n, 0)),
            pl.BlockSpec((WINDOW, kvw), cur),
            pl.BlockSpec((WINDOW, kvw), prev),
            pl.BlockSpec((WINDOW, kvw), cur_v),
            pl.BlockSpec((WINDOW, kvw), prev_v),
        ],
        out_specs=pl.BlockSpec((WINDOW, nq), lambda b, n: (b * nb + n, 0)),
        compiler_params=_params(("parallel", "parallel")),
        name="swa_prompt",
    )(sinks, qkv, qkv, qkv, qkv, qkv)


def _swa_sample_kernel(dec_t, sink_ref, q_ref, kc_ref, vc_ref, kn_ref, vn_ref, o_ref):
    nrow = SWA_G * dec_t
    scale = SWA_HD ** -0.5
    r = lax.broadcasted_iota(I32, (nrow, 2 * WINDOW), 0)
    j = lax.broadcasted_iota(I32, (nrow, 2 * WINDOW), 1)
    tq = _imod(r, dec_t)
    gq = _idiv(r, dec_t)
    dist = WINDOW + tq - j
    valid = (dist >= 0) & (dist < WINDOW)
    distf = dist.astype(F32)
    g1 = _idiv(lax.broadcasted_iota(I32, (nrow, 1), 0), dec_t)
    npad = 2 * WINDOW - WINDOW - kn_ref.shape[1]
    zpad = jnp.zeros((npad, SWA_KV * SWA_HD), F32)
    for s in range(q_ref.shape[0]):
        kall = jnp.concatenate([kc_ref[s], kn_ref[s], zpad], axis=0).astype(BF16)
        vall = jnp.concatenate([vc_ref[s], vn_ref[s], zpad], axis=0).astype(BF16)
        kbd = _block_diag(kall, SWA_KV)
        vbd = _block_diag(vall, SWA_KV)
        sc_all = _dot_nt(q_ref[s].astype(BF16), kbd)
        ps = []
        for kv in range(SWA_KV):
            slope = jnp.zeros((nrow, 2 * WINDOW), F32)
            sink = jnp.zeros((nrow, 1), F32)
            for g in range(SWA_G):
                slope = jnp.where(gq == g, _slope(kv, g), slope)
                sink = jnp.where(g1 == g, sink_ref[kv * SWA_G + g], sink)
            sc = sc_all[:, kv * 2 * WINDOW:(kv + 1) * 2 * WINDOW] * scale - slope * distf
            sc = jnp.where(valid, sc, NEG_INF)
            m = jnp.maximum(jnp.max(sc, axis=-1, keepdims=True), sink)
            p = jnp.exp(sc - m)
            den = jnp.sum(p, axis=-1, keepdims=True) + jnp.exp(sink - m)
            ps.append((p / den).astype(BF16))
        o_ref[s] = _dot(jnp.concatenate(ps, axis=1), vbd).astype(BF16)


def swa_sample_attention(q_s, k_new, v_new, cache_k, cache_v, sinks, dec_t):
    nbatch, nrow, kvw = q_s.shape
    sb = _tile(nbatch, SWA_SAMPLE_SEQS)
    blk3 = lambda r: pl.BlockSpec((sb, r, kvw), lambda i: (i, 0, 0))
    return pl.pallas_call(
        functools.partial(_swa_sample_kernel, dec_t),
        out_shape=jax.ShapeDtypeStruct((nbatch, nrow, kvw), BF16),
        grid=(nbatch // sb,),
        in_specs=[
            pl.BlockSpec(memory_space=pltpu.SMEM),
            blk3(nrow), blk3(WINDOW), blk3(WINDOW), blk3(k_new.shape[1]), blk3(v_new.shape[1]),
        ],
        out_specs=blk3(nrow),
        compiler_params=_params(("parallel",)),
        name="swa_sample",
    )(sinks, q_s, cache_k, cache_v, k_new, v_new)


def _proj_res_kernel(n_prompt_tiles, x_ref, op_ref, os_ref, w_ref, b_ref, xo_ref, wbf_ref):
    i = pl.program_id(0)

    @pl.when(i == 0)
    def _():
        wbf_ref[...] = w_ref[...].astype(BF16)

    o = jnp.where(i < n_prompt_tiles, op_ref[...], os_ref[...])
    xo_ref[...] = x_ref[...] + (_dot(o, wbf_ref[...]) + b_ref[...])


def proj_residual(x, o_prompt, o_sample, w, b):
    t, d = x.shape
    k = w.shape[0]
    tm = _tile(o_sample.shape[0], 512)
    assert o_prompt.shape[0] % tm == 0
    npt = o_prompt.shape[0] // tm
    return pl.pallas_call(
        functools.partial(_proj_res_kernel, npt),
        out_shape=jax.ShapeDtypeStruct((t, d), F32),
        grid=(t // tm,),
        in_specs=[
            pl.BlockSpec((tm, d), lambda i: (i, 0)),
            pl.BlockSpec((tm, k), lambda i: (jnp.minimum(i, npt - 1), 0)),
            pl.BlockSpec((tm, k), lambda i: (jnp.maximum(i - npt, 0), 0)),
            pl.BlockSpec((k, d), lambda i: (0, 0)),
            pl.BlockSpec((1, d), lambda i: (0, 0)),
        ],
        out_specs=pl.BlockSpec((tm, d), lambda i: (i, 0)),
        scratch_shapes=[pltpu.VMEM((k, d), BF16)],
        compiler_params=_params(("arbitrary",)),
        name="proj_residual",
    )(x, o_prompt, o_sample, w, b.reshape(1, d))


def _split3_dot(a, w_hi, w_lo):
    a_hi = a.astype(BF16)
    a_lo = (a - a_hi.astype(F32)).astype(BF16)
    return _dot(a_hi, w_hi) + (_dot(a_hi, w_lo) + _dot(a_lo, w_hi))


def _router_kernel(x_ref, g_ref, w_ref, b_ref, f_ref, meta_ref, cnt_ref, whi_ref, wlo_ref, carry_ref):
    i = pl.program_id(0)
    tm = x_ref.shape[0]

    @pl.when(i == 0)
    def _():
        w = w_ref[...]
        hi = w.astype(BF16)
        whi_ref[...] = hi
        wlo_ref[...] = (w - hi.astype(F32)).astype(BF16)
        carry_ref[...] = jnp.zeros_like(carry_ref)

    f = _rms(x_ref[...], g_ref[...])
    f_ref[...] = f
    logits = _split3_dot(f, whi_ref[...], wlo_ref[...]) + b_ref[...]
    lane = lax.broadcasted_iota(I32, (tm, LANES), 1)
    lanef = lane.astype(F32)
    big = float(LANES)

    def first_max(mask):
        v = jnp.max(jnp.where(mask, logits, NEG_INF), axis=-1, keepdims=True)
        idx = jnp.min(jnp.where(mask & (logits == v), lanef, big), axis=-1, keepdims=True)
        return v, idx

    gmask = lane < N_GROUPS
    gmax, gidx = first_max(gmask)
    gsum = jnp.sum(jnp.where(gmask, jnp.exp(logits - gmax), 0.0), axis=-1, keepdims=True)
    g_gate = 1.0 / gsum
    lo = N_GROUPS + EXPERTS_PER_GROUP * gidx
    emask = (lanef >= lo) & (lanef < lo + EXPERTS_PER_GROUP)
    e1, i1 = first_max(emask)
    e2, i2 = first_max(emask & (lanef != i1))
    z = jnp.exp(e2 - e1)
    gate1 = g_gate / (1.0 + z)
    gate2 = g_gate * z / (1.0 + z)
    id1 = i1 - N_GROUPS
    id2 = i2 - N_GROUPS
    oh1 = (lanef == id1).astype(F32)
    oh2 = (lanef == id2).astype(F32)
    oh = oh1 + oh2
    rr = lax.broadcasted_iota(I32, (tm, tm), 0)
    cc = lax.broadcasted_iota(I32, (tm, tm), 1)
    tri = (cc < rr).astype(BF16)
    before = _dot(tri, oh.astype(BF16)) + carry_ref[...]
    rank1 = jnp.sum(oh1 * before, axis=-1, keepdims=True)
    rank2 = jnp.sum(oh2 * before, axis=-1, keepdims=True)
    carry_ref[...] = carry_ref[...] + jnp.sum(oh, axis=0, keepdims=True)
    cnt_ref[...] = carry_ref[...]
    meta = jnp.zeros((tm, LANES), F32)
    for k, v in enumerate((id1, id2, gate1, gate2, rank1, rank2)):
        meta = jnp.where(lane == k, v, meta)
    meta_ref[...] = meta


def moe_router(x, g, w_group, b_group, w_expert, b_expert):
    t, d = x.shape
    tm = _tile(t, 512)
    w = jnp.zeros((d, LANES), F32).at[:, :N_GROUPS].set(w_group).at[:, N_GROUPS:N_GROUPS + N_EXPERTS].set(w_expert)
    b = jnp.zeros((1, LANES), F32).at[0, :N_GROUPS].set(b_group).at[0, N_GROUPS:N_GROUPS + N_EXPERTS].set(b_expert)
    return pl.pallas_call(
        _router_kernel,
        out_shape=(
            jax.ShapeDtypeStruct((t, d), F32),
            jax.ShapeDtypeStruct((t, LANES), F32),
            jax.ShapeDtypeStruct((1, LANES), F32),
        ),
        grid=(t // tm,),
        in_specs=[
            pl.BlockSpec((tm, d), lambda i: (i, 0)),
            pl.BlockSpec((1, d), lambda i: (0, 0)),
            pl.BlockSpec((d, LANES), lambda i: (0, 0)),
            pl.BlockSpec((1, LANES), lambda i: (0, 0)),
        ],
        out_specs=(
            pl.BlockSpec((tm, d), lambda i: (i, 0)),
            pl.BlockSpec((tm, LANES), lambda i: (i, 0)),
            pl.BlockSpec((1, LANES), lambda i: (0, 0)),
        ),
        scratch_shapes=[pltpu.VMEM((d, LANES), BF16), pltpu.VMEM((d, LANES), BF16), pltpu.VMEM((1, LANES), F32)],
        compiler_params=_params(("arbitrary",)),
        name="moe_router",
    )(x, g.reshape(1, d), w, b)


def _expert_kernel(blk_e_ref, nblk_ref, src_cur_ref, src_nxt_ref, f_hbm, wg_ref, wu_ref, wd_ref,
                   y_ref, xbuf, sem, wg_bf, wu_bf, wd_bf):
    j = pl.program_id(0)
    nb = nblk_ref[0]
    blk = xbuf.shape[1]

    def row_copy(src_ref, slot, r):
        return pltpu.make_async_copy(f_hbm.at[pl.ds(src_ref[0, 0, r], 1)], xbuf.at[slot, pl.ds(r, 1)],
                                     sem.at[slot])

    def gather(src_ref, slot, wait):
        def body(r, carry):
            cp = row_copy(src_ref, slot, r)
            cp.wait() if wait else cp.start()
            return carry
        lax.fori_loop(0, blk, body, 0, unroll=8)

    @pl.when(j == 0)
    def _():
        gather(src_cur_ref, 0, False)

    @pl.when(j + 1 < nb)
    def _():
        gather(src_nxt_ref, (j + 1) % 2, False)

    @pl.when(j < nb)
    def _():
        slot = j % 2
        gather(src_cur_ref, slot, True)
        e = blk_e_ref[j]
        e_prev = blk_e_ref[jnp.maximum(j - 1, 0)]

        @pl.when(jnp.logical_or(j == 0, e != e_prev))
        def _():
            wg_bf[...] = wg_ref[0].astype(BF16)
            wu_bf[...] = wu_ref[0].astype(BF16)
            wd_bf[...] = wd_ref[0].astype(BF16)

        x = xbuf[slot].astype(BF16)
        a = _dot(x, wg_bf[...])
        u = _dot(x, wu_bf[...])
        h = a * (1.0 / (1.0 + jnp.exp(-a))) * u
        y_ref[...] = _dot(h.astype(BF16), wd_bf[...])

    @pl.when(j >= nb)
    def _():
        y_ref[...] = jnp.zeros_like(y_ref)


def moe_experts(f, src, blk_e, nblk, w_gate, w_up, w_down):
    t, d = f.shape
    nslot = src.shape[0]
    nblocks = nslot // MOE_BLOCK
    de = w_gate.shape[2]
    src3 = src.reshape(nblocks, 1, MOE_BLOCK)
    wmap = lambda j, be, nb: (be[j], 0, 0)
    grid_spec = pltpu.PrefetchScalarGridSpec(
        num_scalar_prefetch=2,
        grid=(nblocks,),
        in_specs=[
            pl.BlockSpec((1, 1, MOE_BLOCK), lambda j, be, nb: (j, 0, 0), memory_space=pltpu.SMEM),
            pl.BlockSpec((1, 1, MOE_BLOCK), lambda j, be, nb: (jnp.minimum(j + 1, nblocks - 1), 0, 0),
                         memory_space=pltpu.SMEM),
            pl.BlockSpec(memory_space=pl.ANY),
            pl.BlockSpec((1, d, de), wmap),
            pl.BlockSpec((1, d, de), wmap),
            pl.BlockSpec((1, de, d), wmap),
        ],
        out_specs=pl.BlockSpec((MOE_BLOCK, d), lambda j, be, nb: (j, 0)),
        scratch_shapes=[
            pltpu.VMEM((2, MOE_BLOCK, d), F32),
            pltpu.SemaphoreType.DMA((2,)),
            pltpu.VMEM((d, de), BF16),
            pltpu.VMEM((d, de), BF16),
            pltpu.VMEM((de, d), BF16),
        ],
    )
    return pl.pallas_call(
        _expert_kernel,
        out_shape=jax.ShapeDtypeStruct((nslot, d), F32),
        grid_spec=grid_spec,
        compiler_params=_params(("arbitrary",)),
        name="moe_experts",
    )(blk_e, nblk, src3, src3, f, w_gate, w_up, w_down)


def _combine_kernel(final_norm, d1c_ref, d2c_ref, d1n_ref, d2n_ref, yb_hbm, x_ref, meta_ref, g_ref,
                    o_ref, ybuf, sem):
    i = pl.program_id(0)
    n = pl.num_programs(0)
    tm = x_ref.shape[0]

    def row_copy(d_ref, slot, k, r):
        return pltpu.make_async_copy(yb_hbm.at[pl.ds(d_ref[0, 0, r], 1)], ybuf.at[slot, k, pl.ds(r, 1)],
                                     sem.at[slot])

    def gather(d1_ref, d2_ref, slot, wait):
        def body(r, carry):
            for k, d_ref in enumerate((d1_ref, d2_ref)):
                cp = row_copy(d_ref, slot, k, r)
                cp.wait() if wait else cp.start()
            return carry
        lax.fori_loop(0, tm, body, 0, unroll=8)

    @pl.when(i == 0)
    def _():
        gather(d1c_ref, d2c_ref, 0, False)

    @pl.when(i + 1 < n)
    def _():
        gather(d1n_ref, d2n_ref, (i + 1) % 2, False)

    slot = i % 2
    gather(d1c_ref, d2c_ref, slot, True)
    meta = meta_ref[...]
    gate1 = meta[:, 2:3]
    gate2 = meta[:, 3:4]
    out = x_ref[...] + (ybuf[slot, 0] * gate1 + ybuf[slot, 1] * gate2)
    if final_norm:
        out = _rms(out, g_ref[...])
    o_ref[...] = out


def moe_combine(x, yb, dest, meta, g_final, final_norm):
    t, d = x.shape
    tm = _tile(t, COMBINE_TILE)
    nt = t // tm
    d1 = dest[:, 0].reshape(nt, 1, tm)
    d2 = dest[:, 1].reshape(nt, 1, tm)
    cur = pl.BlockSpec((1, 1, tm), lambda i: (i, 0, 0), memory_space=pltpu.SMEM)
    nxt = pl.BlockSpec((1, 1, tm), lambda i: (jnp.minimum(i + 1, nt - 1), 0, 0), memory_space=pltpu.SMEM)
    return pl.pallas_call(
        functools.partial(_combine_kernel, final_norm),
        out_shape=jax.ShapeDtypeStruct((t, d), F32),
        grid=(nt,),
        in_specs=[
            cur, cur, nxt, nxt,
            pl.BlockSpec(memory_space=pl.ANY),
            pl.BlockSpec((tm, d), lambda i: (i, 0)),
            pl.BlockSpec((tm, LANES), lambda i: (i, 0)),
            pl.BlockSpec((1, d), lambda i: (0, 0)),
        ],
        out_specs=pl.BlockSpec((tm, d), lambda i: (i, 0)),
        scratch_shapes=[pltpu.VMEM((2, TOP_K, tm, d), F32), pltpu.SemaphoreType.DMA((2,))],
        compiler_params=_params(("arbitrary",)),
        name="moe_combine",
    )(d1, d2, d1, d2, yb, x, meta, g_final.reshape(1, d))


def hier_moe_layer(x, norm_g, w_group, b_group, w_expert, b_expert, w_gate, w_up, w_down, g_final, final_norm):
    t, d = x.shape
    f, meta, counts = moe_router(x, norm_g, w_group, b_group, w_expert, b_expert)
    ids = meta[:, 0:2].astype(I32)
    rank = meta[:, 4:6].astype(I32)
    cnt = counts[0, :N_EXPERTS].astype(I32)
    padded = (cnt + MOE_BLOCK - 1) // MOE_BLOCK * MOE_BLOCK
    pend = jnp.cumsum(padded)
    pstart = pend - padded
    dest = pstart[ids] + rank
    nblocks = -(-(t * TOP_K) // MOE_BLOCK) + N_EXPERTS
    nslot = nblocks * MOE_BLOCK
    tok = jnp.broadcast_to(jnp.arange(t, dtype=I32)[:, None], (t, TOP_K))
    src = jnp.zeros((nslot,), I32).at[dest.reshape(-1)].set(tok.reshape(-1))
    blk_start = jnp.arange(nblocks, dtype=I32) * MOE_BLOCK
    blk_e = jnp.minimum(jnp.sum((pend[None, :] <= blk_start[:, None]).astype(I32), axis=1), N_EXPERTS - 1)
    nblk = (pend[-1:] // MOE_BLOCK).astype(I32)
    yb = moe_experts(f, src, blk_e, nblk, w_gate, w_up, w_down)
    return moe_combine(x, yb, dest, meta, g_final, final_norm)


def _mla_down_kernel(x_ref, g_ref, w_ref, gq_ref, gkv_ref, cos_ref, sin_ref, cq_ref, ckr_ref, wbf_ref):
    @pl.when(pl.program_id(0) == 0)
    def _():
        wbf_ref[...] = w_ref[...].astype(BF16)

    h = _rms(x_ref[...], g_ref[...])
    a = _dot(h.astype(BF16), wbf_ref[...])
    q0, c0 = MLA_Q_LORA, MLA_Q_LORA + MLA_KV_LORA
    cq_ref[...] = _rms(a[:, :q0], gq_ref[...]).astype(BF16)
    ckr_ref[:, :MLA_KV_LORA] = _rms(a[:, q0:c0], gkv_ref[...])
    ckr_ref[:, MLA_KV_LORA:] = a[:, c0:c0 + LANES] * cos_ref[...] + a[:, c0 + LANES:] * sin_ref[...]


def _rot_cols(w):
    half = MLA_ROPE // 2
    return jnp.concatenate([-w[..., half:], w[..., :half]], axis=-1)


def mla_down(x, g, w_dqkv, norm_q, norm_kv, cos_t, sin_t):
    t, d = x.shape
    tm = _tile(t, 512)
    q0, c0 = MLA_Q_LORA, MLA_Q_LORA + MLA_KV_LORA
    w_r = w_dqkv[:, c0:]
    zpad = jnp.zeros((d, LANES - MLA_ROPE), F32)
    w = jnp.concatenate([w_dqkv[:, :c0], w_r, zpad, _rot_cols(w_r), zpad], axis=1)
    n = w.shape[1]
    return pl.pallas_call(
        _mla_down_kernel,
        out_shape=(jax.ShapeDtypeStruct((t, q0), BF16), jax.ShapeDtypeStruct((t, MLA_KV_LORA + LANES), F32)),
        grid=(t // tm,),
        in_specs=[
            pl.BlockSpec((tm, d), lambda i: (i, 0)),
            pl.BlockSpec((1, d), lambda i: (0, 0)),
            pl.BlockSpec((d, n), lambda i: (0, 0)),
            pl.BlockSpec((1, q0), lambda i: (0, 0)),
            pl.BlockSpec((1, MLA_KV_LORA), lambda i: (0, 0)),
            pl.BlockSpec((tm, LANES), lambda i: (i, 0)),
            pl.BlockSpec((tm, LANES), lambda i: (i, 0)),
        ],
        out_specs=(pl.BlockSpec((tm, q0), lambda i: (i, 0)),
                   pl.BlockSpec((tm, MLA_KV_LORA + LANES), lambda i: (i, 0))),
        scratch_shapes=[pltpu.VMEM((d, n), BF16)],
        compiler_params=_params(("arbitrary",)),
        name="mla_down",
    )(x, g.reshape(1, d), w, norm_q.reshape(1, q0), norm_kv.reshape(1, MLA_KV_LORA), cos_t, sin_t)


def _mla_up_kernel(cq_ref, ckr_ref, qc_ref, qs_ref, wq_ref, wk_ref, wv_ref, q_ref, k_ref, v_ref,
                   wq_bf, wk_bf, wv_bf):
    @pl.when(pl.program_id(0) == 0)
    def _():
        wq_bf[...] = wq_ref[...].astype(BF16)
        wk_bf[...] = wk_ref[...].astype(BF16)
        wv_bf[...] = wv_ref[...].astype(BF16)

    nq = q_ref.shape[1]
    ab = _dot(cq_ref[...], wq_bf[...])
    qc = qc_ref[...]
    qs = qs_ref[...]
    for h in range(MLA_HEADS):
        sl = slice(h * LANES, (h + 1) * LANES)
        sl_b = slice(nq + h * LANES, nq + (h + 1) * LANES)
        q_ref[:, sl] = (ab[:, sl] * qc + ab[:, sl_b] * qs).astype(BF16)
    ckr = ckr_ref[...].astype(BF16)
    k_ref[...] = _dot(ckr, wk_bf[...]).astype(BF16)
    v_ref[...] = _dot(ckr[:, :MLA_KV_LORA], wv_bf[...]).astype(BF16)


def _head_slots(w, lo):
    r, h, n = w.shape
    out = jnp.zeros((r, h, LANES), F32).at[:, :, lo:lo + n].set(w)
    return out.reshape(r, h * LANES)


def mla_up(cq, ckr, q_cos, q_sin, w_uq, w_uk, w_uv):
    t = cq.shape[0]
    tm = _tile(t, 256)
    nq = MLA_HEADS * LANES
    kin = MLA_KV_LORA + LANES
    wq3 = w_uq.reshape(MLA_Q_LORA, MLA_HEADS, MLA_NOPE + MLA_ROPE)
    wq_a = _head_slots(wq3, 0)
    wq_b = _head_slots(_rot_cols(wq3[:, :, MLA_NOPE:]), MLA_NOPE)
    wq = jnp.concatenate([wq_a, wq_b], axis=1)
    place = jnp.zeros((LANES, MLA_HEADS, LANES), F32)
    place = place.at[jnp.arange(MLA_ROPE), :, MLA_NOPE + jnp.arange(MLA_ROPE)].set(1.0)
    wk = jnp.concatenate([_head_slots(w_uk, 0), place.reshape(LANES, nq)], axis=0)
    even = (jnp.arange(MLA_HEADS) % 2 == 0)[None, :, None]
    wv = jnp.where(even, _head_slots(w_uv, 0).reshape(MLA_KV_LORA, MLA_HEADS, LANES),
                   _head_slots(w_uv, MLA_V).reshape(MLA_KV_LORA, MLA_HEADS, LANES)).reshape(MLA_KV_LORA, nq)
    full = lambda r, c: pl.BlockSpec((r, c), lambda i: (0, 0))
    rows = lambda c: pl.BlockSpec((tm, c), lambda i: (i, 0))
    return pl.pallas_call(
        _mla_up_kernel,
        out_shape=tuple(jax.ShapeDtypeStruct((t, nq), BF16) for _ in range(3)),
        grid=(t // tm,),
        in_specs=[rows(MLA_Q_LORA), rows(kin), rows(LANES), rows(LANES),
                  full(MLA_Q_LORA, 2 * nq), full(kin, nq), full(MLA_KV_LORA, nq)],
        out_specs=(rows(nq), rows(nq), rows(nq)),
        scratch_shapes=[pltpu.VMEM((MLA_Q_LORA, 2 * nq), BF16), pltpu.VMEM((kin, nq), BF16),
                        pltpu.VMEM((MLA_KV_LORA, nq), BF16)],
        compiler_params=_params(("arbitrary",)),
        name="mla_up",
    )(cq, ckr, q_cos, q_sin, wq, wk, wv)


def _mla_flash_kernel(q_ref, k_ref, v_ref, o_ref, m_ref, l_ref, acc_ref):
    qi = pl.program_id(2)
    tq = q_ref.shape[0]
    lane = lax.broadcasted_iota(I32, (tq, LANES), 1)
    first_half = lane < MLA_V
    m_ref[...] = jnp.full_like(m_ref, NEG_INF)
    l_ref[...] = jnp.zeros_like(l_ref)
    acc_ref[...] = jnp.zeros_like(acc_ref)
    q = q_ref[...]

    def step(kb, masked):
        start = pl.multiple_of(kb * tq, tq)
        kblk = k_ref[pl.ds(start, tq), :]
        vblk = v_ref[pl.ds(start, tq), :]
        pv = None
        alphas = []
        for h in range(2):
            s = _dot_nt(q[:, h * LANES:(h + 1) * LANES], kblk[:, h * LANES:(h + 1) * LANES])
            if masked:
                row = lax.broadcasted_iota(I32, s.shape, 0)
                col = lax.broadcasted_iota(I32, s.shape, 1)
                s = jnp.where(col <= row, s, NEG_INF)
            sc = [s[:, c * LANES:(c + 1) * LANES] for c in range(tq // LANES)]
            m_old = m_ref[h]
            m_new = jnp.maximum(m_old, jnp.max(functools.reduce(jnp.maximum, sc), axis=-1, keepdims=True))
            alpha = jnp.exp2(m_old - m_new)
            pc = [jnp.exp2(x - m_new) for x in sc]
            l_ref[h] = alpha * l_ref[h] + functools.reduce(jnp.add, pc)
            m_ref[h] = m_new
            p = jnp.concatenate([x.astype(BF16) for x in pc], axis=1)
            part = _dot(p, vblk[:, h * LANES:(h + 1) * LANES])
            pv = part if pv is None else pv + part
            alphas.append(alpha)
        acc_ref[...] = jnp.where(first_half, alphas[0], alphas[1]) * acc_ref[...] + pv

    def body(kb, carry):
        step(kb, False)
        return carry

    lax.fori_loop(0, qi, body, 0)
    step(qi, True)
    l0 = jnp.sum(l_ref[0], axis=-1, keepdims=True)
    l1 = jnp.sum(l_ref[1], axis=-1, keepdims=True)
    o_ref[...] = (acc_ref[...] * jnp.where(first_half, 1.0 / l0, 1.0 / l1)).astype(BF16)


def mla_prompt_attention(qp, kp, vp, batch, seq):
    tq = _tile(seq, MLA_TQ)
    nqb = seq // tq
    npair = MLA_HEADS // 2
    pw = 2 * LANES
    return pl.pallas_call(
        _mla_flash_kernel,
        out_shape=jax.ShapeDtypeStruct((batch * seq, MLA_HEADS * MLA_V), BF16),
        grid=(batch, npair, nqb),
        in_specs=[
            pl.BlockSpec((tq, pw), lambda b, h, i: (b * nqb + i, h)),
            pl.BlockSpec((seq, pw), lambda b, h, i: (b, h)),
            pl.BlockSpec((seq, pw), lambda b, h, i: (b, h)),
        ],
        out_specs=pl.BlockSpec((tq, LANES), lambda b, h, i: (b * nqb + i, h)),
        scratch_shapes=[pltpu.VMEM((2, tq, LANES), F32), pltpu.VMEM((2, tq, LANES), F32),
                        pltpu.VMEM((tq, LANES), F32)],
        compiler_params=_params(("parallel", "parallel", "arbitrary")),
        name="mla_prompt",
    )(qp, kp, vp)


def _mla_absorb_kernel(q_ref, w_ref, o_ref):
    o_ref[0] = _dot(q_ref[...], w_ref[0].astype(BF16)).astype(BF16)


def mla_absorb(q_s, w_uk):
    ns = q_s.shape[0]
    kin = MLA_KV_LORA + LANES
    w = jnp.zeros((MLA_HEADS, LANES, kin), F32)
    w = w.at[:, :MLA_NOPE, :MLA_KV_LORA].set(jnp.transpose(w_uk, (1, 2, 0)))
    w = w.at[:, MLA_NOPE + jnp.arange(MLA_ROPE), MLA_KV_LORA + jnp.arange(MLA_ROPE)].set(1.0)
    return pl.pallas_call(
        _mla_absorb_kernel,
        out_shape=jax.ShapeDtypeStruct((MLA_HEADS, ns, kin), BF16),
        grid=(MLA_HEADS,),
        in_specs=[pl.BlockSpec((ns, LANES), lambda h: (0, h)),
                  pl.BlockSpec((1, LANES, kin), lambda h: (h, 0, 0))],
        out_specs=pl.BlockSpec((1, ns, kin), lambda h: (h, 0, 0)),
        compiler_params=_params(("parallel",)),
        name="mla_absorb",
    )(q_s, w)


def _mla_decode_kernel(npg, dec_t, pt_ref, q_ref, new_ref, lat_hbm, krt_hbm, o_ref,
                       lat_buf, krt_buf, sem, s_buf, latb_buf):
    b = pl.program_id(0)
    nb = pl.num_programs(0)
    npages = pt_ref.shape[1]
    page = lat_hbm.shape[1]
    ck = npg * page
    nchunk = npages // npg
    nrow = q_ref.shape[0]

    def page_copies(bb, slot, i):
        pg = pt_ref[bb, i]
        return (pltpu.make_async_copy(lat_hbm.at[pg], lat_buf.at[slot, pl.ds(i * page, page)], sem.at[slot]),
                pltpu.make_async_copy(krt_hbm.at[pg], krt_buf.at[slot, pl.ds(i * MLA_ROPE, MLA_ROPE)],
                                      sem.at[slot]))

    def fetch(bb, slot, wait):
        def body(i, carry):
            for cp in page_copies(bb, slot, i):
                cp.wait() if wait else cp.start()
            return carry
        lax.fori_loop(0, npages, body, 0, unroll=8)

    @pl.when(b == 0)
    def _():
        fetch(0, 0, False)

    @pl.when(b + 1 < nb)
    def _():
        fetch(b + 1, (b + 1) % 2, False)

    slot = b % 2
    fetch(b, slot, True)

    q = q_ref[...]
    ql = q[:, :MLA_KV_LORA]
    qr = q[:, MLA_KV_LORA:MLA_KV_LORA + MLA_ROPE]
    mloc = jnp.full((nrow, LANES), NEG_INF, F32)
    for c in range(nchunk):
        lat = lat_buf[slot, c * ck:(c + 1) * ck, :].astype(BF16)
        latb_buf[c * ck:(c + 1) * ck, :] = lat
        krt = jnp.concatenate(
            [krt_buf[slot, (c * npg + i) * MLA_ROPE:(c * npg + i + 1) * MLA_ROPE, :] for i in range(npg)],
            axis=1).astype(BF16)
        s = _dot_nt(ql, lat) + _dot(qr, krt)
        s_buf[:, c * ck:(c + 1) * ck] = s
        for j in range(ck // LANES):
            mloc = jnp.maximum(mloc, s[:, j * LANES:(j + 1) * LANES])

    new = new_ref[...].astype(BF16)
    sn = _dot_nt(q, new)
    t_row = _idiv(lax.broadcasted_iota(I32, sn.shape, 0), MLA_HEADS)
    t_col = lax.broadcasted_iota(I32, sn.shape, 1)
    sn = jnp.where(t_col <= t_row, sn, NEG_INF)
    m = jnp.maximum(jnp.max(mloc, axis=-1, keepdims=True), jnp.max(sn, axis=-1, keepdims=True))
    pn = jnp.exp2(sn - m)
    lsum = jnp.zeros((nrow, LANES), F32)
    acc = jnp.zeros((nrow, MLA_KV_LORA), F32)
    for c in range(nchunk):
        pc = [jnp.exp2(s_buf[:, c * ck + j * LANES:c * ck + (j + 1) * LANES] - m) for j in range(ck // LANES)]
        lsum = lsum + functools.reduce(jnp.add, pc)
        p = jnp.concatenate([x.astype(BF16) for x in pc], axis=1)
        acc = acc + _dot(p, latb_buf[c * ck:(c + 1) * ck, :])
    l = jnp.sum(lsum, axis=-1, keepdims=True) + jnp.sum(pn, axis=-1, keepdims=True)
    pnb = pn.astype(BF16).astype(F32)
    cn = new[:, :MLA_KV_LORA].astype(F32)
    for t in range(dec_t):
        acc = acc + pnb[:, t:t + 1] * cn[t:t + 1, :]
    o_ref[...] = (acc / l).astype(BF16)


def mla_decode(qabs, ckr_new, cache_lat, cache_krt, page_table, dec_t):
    nbatch, npages = page_table.shape
    npg = _tile(npages, MLA_PAGES_PER_STEP)
    nrow = dec_t * MLA_HEADS
    kin = MLA_KV_LORA + LANES
    page = cache_lat.shape[1]
    nkeys = npages * page
    grid_spec = pltpu.PrefetchScalarGridSpec(
        num_scalar_prefetch=1,
        grid=(nbatch,),
        in_specs=[pl.BlockSpec((nrow, kin), lambda b, pt: (b, 0)),
                  pl.BlockSpec((None, 8, kin), lambda b, pt: (b, 0, 0)),
                  pl.BlockSpec(memory_space=pl.ANY),
                  pl.BlockSpec(memory_space=pl.ANY)],
        out_specs=pl.BlockSpec((nrow, MLA_KV_LORA), lambda b, pt: (b, 0)),
        scratch_shapes=[pltpu.VMEM((2, nkeys, MLA_KV_LORA), F32),
                        pltpu.VMEM((2, npages * MLA_ROPE, page), F32),
                        pltpu.SemaphoreType.DMA((2,)),
                        pltpu.VMEM((nrow, nkeys), F32),
                        pltpu.VMEM((nkeys, MLA_KV_LORA), BF16)],
    )
    return pl.pallas_call(
        functools.partial(_mla_decode_kernel, npg, dec_t),
        out_shape=jax.ShapeDtypeStruct((nbatch * nrow, MLA_KV_LORA), BF16),
        grid_spec=grid_spec,
        compiler_params=_params(("arbitrary",)),
        name="mla_decode",
    )(page_table, qabs, ckr_new, cache_lat, cache_krt)


def _mla_unabsorb_kernel(ol_ref, w_ref, o_ref):
    r = _dot(ol_ref[...], w_ref[...].astype(BF16))
    row_h = _imod(lax.broadcasted_iota(I32, r.shape, 0), MLA_HEADS)
    col_h = _idiv(lax.broadcasted_iota(I32, r.shape, 1), MLA_V)
    r = jnp.where(row_h == col_h, r, 0.0)
    o_ref[...] = jnp.sum(r.reshape(r.shape[0] // MLA_HEADS, MLA_HEADS, r.shape[1]), axis=1).astype(BF16)


def mla_unabsorb(o_lat, w_uv):
    rows = o_lat.shape[0]
    tr = _tile(rows, 1024)
    n = MLA_HEADS * MLA_V
    return pl.pallas_call(
        _mla_unabsorb_kernel,
        out_shape=jax.ShapeDtypeStruct((rows // MLA_HEADS, n), BF16),
        grid=(rows // tr,),
        in_specs=[pl.BlockSpec((tr, MLA_KV_LORA), lambda i: (i, 0)),
                  pl.BlockSpec((MLA_KV_LORA, n), lambda i: (0, 0))],
        out_specs=pl.BlockSpec((tr // MLA_HEADS, n), lambda i: (i, 0)),
        compiler_params=_params(("parallel",)),
        name="mla_unabsorb",
    )(o_lat, w_uv.reshape(MLA_KV_LORA, n))


def _rope_tables(pos):
    half = MLA_ROPE // 2
    inv = jnp.power(ROPE_THETA, -jnp.arange(half, dtype=F32) * 2.0 / MLA_ROPE)
    ang = pos[:, None] * inv[None, :]
    cos2 = jnp.tile(jnp.cos(ang), (1, 2))
    sin2 = jnp.tile(jnp.sin(ang), (1, 2))
    t = pos.shape[0]
    scale = (MLA_NOPE + MLA_ROPE) ** -0.5 * 1.4426950408889634
    k_cos = jnp.zeros((t, LANES), F32).at[:, :MLA_ROPE].set(cos2)
    k_sin = jnp.zeros((t, LANES), F32).at[:, :MLA_ROPE].set(sin2)
    q_cos = jnp.zeros((t, LANES), F32).at[:, :MLA_NOPE].set(scale).at[:, MLA_NOPE:MLA_NOPE + MLA_ROPE].set(scale * cos2)
    q_sin = jnp.zeros((t, LANES), F32).at[:, MLA_NOPE:MLA_NOPE + MLA_ROPE].set(scale * sin2)
    return k_cos, k_sin, q_cos, q_sin


def kernel(x_prompt, x_sample, cache_swa_k, cache_swa_v, cache_mla_latent, cache_mla_krope, page_table,
           norm_attn, norm_ffn, norm_final,
           swa_w_qkv, swa_b_qkv, swa_sinks, swa_w_o, swa_b_o,
           mla_w_dqkv, mla_norm_q, mla_norm_kv, mla_w_uq, mla_w_uk, mla_w_uv, mla_w_o,
           moe_w_group, moe_b_group, moe_w_expert, moe_b_expert, moe_w_gate, moe_w_up, moe_w_down):
    batch, seq, d = x_prompt.shape
    nbatch, dec_t, _ = x_sample.shape
    n_p = batch * seq
    n_s = nbatch * dec_t
    npages = page_table.shape[1]
    page = cache_mla_latent.shape[2]
    past_len = npages * page
    nq = SWA_HEADS * SWA_HD
    kvw = SWA_KV * SWA_HD
    x = jnp.concatenate([x_prompt.reshape(n_p, d), x_sample.reshape(n_s, d)], axis=0)

    w_qkv = swa_w_qkv[0]
    w_q = w_qkv[:, :nq].reshape(d, SWA_KV, SWA_G, SWA_HD).transpose(0, 2, 1, 3).reshape(d, nq)
    b_q = swa_b_qkv[0][:nq].reshape(SWA_KV, SWA_G, SWA_HD).transpose(1, 0, 2).reshape(nq)
    w0 = jnp.concatenate([w_q, w_qkv[:, nq:]], axis=1)
    b0 = jnp.concatenate([b_q, swa_b_qkv[0][nq:]])
    w_o0 = swa_w_o[0].reshape(SWA_KV, SWA_G, SWA_HD, d).transpose(1, 0, 2, 3).reshape(nq, d)
    qkv = norm_proj(x, norm_attn[0], w0, b0)
    o_p = swa_prompt_attention(qkv, swa_sinks[0], batch, seq)
    qkv_s = qkv[n_p:]
    q_s = qkv_s[:, :nq].reshape(nbatch, dec_t, SWA_G, kvw).transpose(0, 2, 1, 3).reshape(nbatch, SWA_G * dec_t, kvw)
    k_s = qkv_s[:, nq:nq + kvw].reshape(nbatch, dec_t, kvw)
    v_s = qkv_s[:, nq + kvw:].reshape(nbatch, dec_t, kvw)
    pad8 = lambda a: jnp.pad(a, ((0, 0), (0, 8 - dec_t), (0, 0)))
    ck = cache_swa_k[0].reshape(nbatch, WINDOW, kvw)
    cv = cache_swa_v[0].reshape(nbatch, WINDOW, kvw)
    o_s = swa_sample_attention(q_s, pad8(k_s), pad8(v_s), ck, cv, swa_sinks[0], dec_t)
    o_s = o_s.reshape(nbatch, SWA_G, dec_t, kvw).transpose(0, 2, 1, 3).reshape(n_s, nq)
    x = proj_residual(x, o_p, o_s, w_o0, swa_b_o[0])
    x = hier_moe_layer(x, norm_ffn[0], moe_w_group[0], moe_b_group[0], moe_w_expert[0], moe_b_expert[0],
                       moe_w_gate[0], moe_w_up[0], moe_w_down[0], norm_final, False)

    k_p = qkv[:n_p, nq:nq + kvw].reshape(batch, seq, SWA_KV, SWA_HD)
    v_p = qkv[:n_p, nq + kvw:].reshape(batch, seq, SWA_KV, SWA_HD)
    swa_kp = k_p[:, seq - WINDOW:][None]
    swa_vp = v_p[:, seq - WINDOW:][None]
    swa_ks = jnp.concatenate([cache_swa_k[0], k_s.reshape(nbatch, dec_t, SWA_KV, SWA_HD)], axis=1)[:, dec_t:][None]
    swa_vs = jnp.concatenate([cache_swa_v[0], v_s.reshape(nbatch, dec_t, SWA_KV, SWA_HD)], axis=1)[:, dec_t:][None]

    pos = jnp.concatenate([jnp.tile(jnp.arange(seq, dtype=F32), batch),
                           jnp.tile(past_len + jnp.arange(dec_t, dtype=F32), nbatch)])
    k_cos, k_sin, q_cos, q_sin = _rope_tables(pos)
    cq, ckr = mla_down(x, norm_attn[1], mla_w_dqkv[0], mla_norm_q[0], mla_norm_kv[0], k_cos, k_sin)
    qp, kp, vp = mla_up(cq, ckr, q_cos, q_sin, mla_w_uq[0], mla_w_uk[0], mla_w_uv[0])
    o_p = mla_prompt_attention(qp, kp, vp, batch, seq)
    qabs = mla_absorb(qp[n_p:], mla_w_uk[0])
    qabs = qabs.transpose(1, 0, 2).reshape(n_s * MLA_HEADS, MLA_KV_LORA + LANES)
    ckr_new = jnp.pad(ckr[n_p:].reshape(nbatch, dec_t, MLA_KV_LORA + LANES), ((0, 0), (0, 8 - dec_t), (0, 0)))
    o_lat = mla_decode(qabs, ckr_new, cache_mla_latent[0], jnp.swapaxes(cache_mla_krope[0], 1, 2), page_table, dec_t)
    o_s = mla_unabsorb(o_lat, mla_w_uv[0])
    x = proj_residual(x, o_p, o_s, mla_w_o[0], jnp.zeros((d,), F32))
    y = hier_moe_layer(x, norm_ffn[1], moe_w_group[1], moe_b_group[1], moe_w_expert[1], moe_b_expert[1],
                       moe_w_gate[1], moe_w_up[1], moe_w_down[1], norm_final, True)

    c_all = ckr[:, :MLA_KV_LORA]
    r_all = ckr[:, MLA_KV_LORA:MLA_KV_LORA + MLA_ROPE]
    return (y[:n_p].reshape(batch, seq, d), y[n_p:].reshape(nbatch, dec_t, d),
            swa_kp, swa_vp, swa_ks, swa_vs,
            c_all[:n_p].reshape(1, batch, seq, MLA_KV_LORA), r_all[:n_p].reshape(1, batch, seq, MLA_ROPE),
            c_all[n_p:].reshape(1, nbatch, dec_t, MLA_KV_LORA), r_all[n_p:].reshape(1, nbatch, dec_t, MLA_ROPE))
```

```python
import functools

import jax
import jax.numpy as jnp
import numpy as np
from jax import lax
from jax.experimental import pallas as pl
from jax.experimental.pallas import tpu as pltpu

F32 = jnp.float32
BF16 = jnp.bfloat16
I32 = jnp.int32

SWA_HEADS = 16
SWA_KV = 4
SWA_G = SWA_HEADS // SWA_KV
SWA_HD = 64
WINDOW = 128
MLA_HEADS = 16
MLA_Q_LORA = 384
MLA_KV_LORA = 256
MLA_NOPE = 64
MLA_ROPE = 32
MLA_V = 64
ROPE_THETA = 10000.0
N_GROUPS = 8
EXPERTS_PER_GROUP = 8
N_EXPERTS = N_GROUPS * EXPERTS_PER_GROUP
TOP_K = 2
RMS_EPS = 1e-6
NEG_INF = -1e30
LOG2E = 1.4426950408889634

LANES = 128
VMEM_LIMIT_BYTES = 52 * 1024 * 1024

MOE_BLOCK = 256
COMBINE_TILE = 256
MLA_TQ = 1024
MLA_PAGES_PER_STEP = 8
SWA_SAMPLE_SEQS = 8


def _tile(n, pref):
    t = pref
    while t > 8 and n % t:
        t //= 2
    assert n % t == 0, (n, pref)
    return t


def _params(sem, vmem=VMEM_LIMIT_BYTES):
    return pltpu.CompilerParams(dimension_semantics=sem, vmem_limit_bytes=vmem)


def _log2(n):
    assert n > 0 and n & (n - 1) == 0, n
    return n.bit_length() - 1


def _idiv(v, n):
    return lax.shift_right_logical(v, _log2(n))


def _imod(v, n):
    assert n & (n - 1) == 0, n
    return jnp.bitwise_and(v, n - 1)


def _rms(x, g):
    return x * lax.rsqrt(jnp.mean(x * x, axis=-1, keepdims=True) + RMS_EPS) * g


def _dot(a, b):
    return jnp.dot(a, b, preferred_element_type=F32)


def _dot_nt(a, b):
    return lax.dot_general(a, b, (((1,), (1,)), ((), ())), preferred_element_type=F32)


def _split_specs(tm, width, n_first_tiles):
    return (pl.BlockSpec((tm, width), lambda i: (jnp.minimum(i, n_first_tiles - 1), 0)),
            pl.BlockSpec((tm, width), lambda i: (jnp.maximum(i - n_first_tiles, 0), 0)))


def _norm_proj_kernel(n_prompt_tiles, xp_ref, xs_ref, g_ref, w_ref, b_ref, o_ref, wbf_ref):
    i = pl.program_id(0)

    @pl.when(i == 0)
    def _():
        wbf_ref[...] = w_ref[...].astype(BF16)

    x = jnp.where(i < n_prompt_tiles, xp_ref[...], xs_ref[...])
    h = _rms(x, g_ref[...])
    o_ref[...] = _dot(h.astype(BF16), wbf_ref[...]) + b_ref[...]


def norm_proj(x_p, x_s, g, w, b):
    d = x_p.shape[1]
    t = x_p.shape[0] + x_s.shape[0]
    n = w.shape[1]
    tm = _tile(x_s.shape[0], 512)
    assert x_p.shape[0] % tm == 0
    npt = x_p.shape[0] // tm
    return pl.pallas_call(
        functools.partial(_norm_proj_kernel, npt),
        out_shape=jax.ShapeDtypeStruct((t, n), F32),
        grid=(t // tm,),
        in_specs=[
            *_split_specs(tm, d, npt),
            pl.BlockSpec((1, d), lambda i: (0, 0)),
            pl.BlockSpec((d, n), lambda i: (0, 0)),
            pl.BlockSpec((1, n), lambda i: (0, 0)),
        ],
        out_specs=pl.BlockSpec((tm, n), lambda i: (i, 0)),
        scratch_shapes=[pltpu.VMEM((d, n), BF16)],
        compiler_params=_params(("arbitrary",)),
        name="norm_proj",
    )(x_p, x_s, g.reshape(1, d), w, b.reshape(1, n))


def _block_diag(x, nseg):
    w = x.shape[1] // nseg
    seg = _idiv(lax.broadcasted_iota(I32, x.shape, 1), w)
    zero = jnp.zeros_like(x)
    return jnp.concatenate([jnp.where(seg == s, x, zero) for s in range(nseg)], axis=0)


def _slope(kv, g):
    return 2.0 ** (-8.0 * (kv * SWA_G + g + 1) / SWA_HEADS)


def _swa_bias_tables():
    qi = np.arange(WINDOW)[:, None]
    kj = np.arange(2 * WINDOW)[None, :]
    dist = qi + WINDOW - kj
    band = (dist >= 0) & (dist < WINDOW)
    out = np.empty((2, SWA_G * SWA_KV, WINDOW, 2 * WINDOW), np.float32)
    for g in range(SWA_G):
        for kv in range(SWA_KV):
            bias = -_slope(kv, g) * dist * LOG2E
            out[1, g * SWA_KV + kv] = np.where(band, bias, NEG_INF)
            out[0, g * SWA_KV + kv] = np.where(band & (kj >= WINDOW), bias, NEG_INF)
    return out


def _swa_prompt_kernel(sink_ref, bias_ref, q_ref, kc_ref, kp_ref, vc_ref, vp_ref, o_ref):
    kvw = SWA_KV * SWA_HD
    k2 = jnp.concatenate([kp_ref[...], kc_ref[...]], axis=0).astype(BF16)
    v2 = jnp.concatenate([vp_ref[...], vc_ref[...]], axis=0).astype(BF16)
    kbd = _block_diag(k2, SWA_KV)
    vbd = _block_diag(v2, SWA_KV)
    for g in range(SWA_G):
        qg = (q_ref[:, g * kvw:(g + 1) * kvw] * (SWA_HD ** -0.5 * LOG2E)).astype(BF16)
        s = _dot_nt(qg, kbd)
        ps = []
        for kv in range(SWA_KV):
            sink = sink_ref[kv * SWA_G + g] * LOG2E
            sc = s[:, kv * 2 * WINDOW:(kv + 1) * 2 * WINDOW] + bias_ref[g * SWA_KV + kv]
            m = jnp.maximum(jnp.max(sc, axis=-1, keepdims=True), sink)
            p = jnp.exp2(sc - m)
            den = jnp.sum(p, axis=-1, keepdims=True) + jnp.exp2(sink - m)
            ps.append((p * (1.0 / den)).astype(BF16))
        og = _dot(jnp.concatenate(ps, axis=1), vbd)
        o_ref[:, g * kvw:(g + 1) * kvw] = og.astype(BF16)


def swa_prompt_attention(qkv, sinks, batch, seq):
    nb = seq // WINDOW
    nq = SWA_HEADS * SWA_HD
    kvw = SWA_KV * SWA_HD
    kcol = nq // kvw
    bias = jnp.asarray(_swa_bias_tables())
    cur = lambda b, n: (b * nb + n, kcol)
    prev = lambda b, n: (b * nb + jnp.maximum(n - 1, 0), kcol)
    cur_v = lambda b, n: (b * nb + n, kcol + 1)
    prev_v = lambda b, n: (b * nb + jnp.maximum(n - 1, 0), kcol + 1)
    return pl.pallas_call(
        _swa_prompt_kernel,
        out_shape=jax.ShapeDtypeStruct((batch * seq, nq), BF16),
        grid=(batch, nb),
        in_specs=[
            pl.BlockSpec(memory_space=pltpu.SMEM),
            pl.BlockSpec((None,) + bias.shape[1:], lambda b, n: (jnp.minimum(n, 1), 0, 0, 0)),
            pl.BlockSpec((WINDOW, nq), lambda b, n: (b * nb + n, 0)),
            pl.BlockSpec((WINDOW, kvw), cur),
            pl.BlockSpec((WINDOW, kvw), prev),
            pl.BlockSpec((WINDOW, kvw), cur_v),
            pl.BlockSpec((WINDOW, kvw), prev_v),
        ],
        out_specs=pl.BlockSpec((WINDOW, nq), lambda b, n: (b * nb + n, 0)),
        compiler_params=_params(("parallel", "parallel")),
        name="swa_prompt",
    )(sinks, bias, qkv, qkv, qkv, qkv, qkv)


def _swa_sample_kernel(dec_t, sink_ref, q_ref, kc_ref, vc_ref, kn_ref, vn_ref, o_ref):
    nrow = SWA_G * dec_t
    scale = SWA_HD ** -0.5
    r = lax.broadcasted_iota(I32, (nrow, 2 * WINDOW), 0)
    j = lax.broadcasted_iota(I32, (nrow, 2 * WINDOW), 1)
    tq = _imod(r, dec_t)
    gq = _idiv(r, dec_t)
    dist = WINDOW + tq - j
    valid = (dist >= 0) & (dist < WINDOW)
    distf = dist.astype(F32)
    g1 = _idiv(lax.broadcasted_iota(I32, (nrow, 1), 0), dec_t)
    npad = 2 * WINDOW - WINDOW - kn_ref.shape[1]
    zpad = jnp.zeros((npad, SWA_KV * SWA_HD), F32)
    for s in range(q_ref.shape[0]):
        kall = jnp.concatenate([kc_ref[s], kn_ref[s], zpad], axis=0).astype(BF16)
        vall = jnp.concatenate([vc_ref[s], vn_ref[s], zpad], axis=0).astype(BF16)
        kbd = _block_diag(kall, SWA_KV)
        vbd = _block_diag(vall, SWA_KV)
        sc_all = _dot_nt(q_ref[s].astype(BF16), kbd)
        ps = []
        for kv in range(SWA_KV):
            slope = jnp.zeros((nrow, 2 * WINDOW), F32)
            sink = jnp.zeros((nrow, 1), F32)
            for g in range(SWA_G):
                slope = jnp.where(gq == g, _slope(kv, g), slope)
                sink = jnp.where(g1 == g, sink_ref[kv * SWA_G + g], sink)
            sc = sc_all[:, kv * 2 * WINDOW:(kv + 1) * 2 * WINDOW] * scale - slope * distf
            sc = jnp.where(valid, sc, NEG_INF)
            m = jnp.maximum(jnp.max(sc, axis=-1, keepdims=True), sink)
            p = jnp.exp(sc - m)
            den = jnp.sum(p, axis=-1, keepdims=True) + jnp.exp(sink - m)
            ps.append((p / den).astype(BF16))
        o_ref[s] = _dot(jnp.concatenate(ps, axis=1), vbd).astype(BF16)


def swa_sample_attention(q_s, k_new, v_new, cache_k, cache_v, sinks, dec_t):
    nbatch, nrow, kvw = q_s.shape
    sb = _tile(nbatch, SWA_SAMPLE_SEQS)
    blk3 = lambda r: pl.BlockSpec((sb, r, kvw), lambda i: (i, 0, 0))
    return pl.pallas_call(
        functools.partial(_swa_sample_kernel, dec_t),
        out_shape=jax.ShapeDtypeStruct((nbatch, nrow, kvw), BF16),
        grid=(nbatch // sb,),
        in_specs=[
            pl.BlockSpec(memory_space=pltpu.SMEM),
            blk3(nrow), blk3(WINDOW), blk3(WINDOW), blk3(k_new.shape[1]), blk3(v_new.shape[1]),
        ],
        out_specs=blk3(nrow),
        compiler_params=_params(("parallel",)),
        name="swa_sample",
    )(sinks, q_s, cache_k, cache_v, k_new, v_new)


def _proj_res_kernel(n_prompt_tiles, split_x, *refs):
    if split_x:
        xp_ref, xs_ref, op_ref, os_ref, w_ref, b_ref, xo_ref, wbf_ref = refs
    else:
        x_ref, op_ref, os_ref, w_ref, b_ref, xo_ref, wbf_ref = refs
    i = pl.program_id(0)

    @pl.when(i == 0)
    def _():
        wbf_ref[...] = w_ref[...].astype(BF16)

    is_prompt = i < n_prompt_tiles
    x = jnp.where(is_prompt, xp_ref[...], xs_ref[...]) if split_x else x_ref[...]
    o = jnp.where(is_prompt, op_ref[...], os_ref[...])
    xo_ref[...] = x + (_dot(o, wbf_ref[...]) + b_ref[...])


def proj_residual(x, o_prompt, o_sample, w, b):
    split_x = isinstance(x, tuple)
    k, d = w.shape
    t = o_prompt.shape[0] + o_sample.shape[0]
    tm = _tile(o_sample.shape[0], 512)
    assert o_prompt.shape[0] % tm == 0
    npt = o_prompt.shape[0] // tm
    x_specs = list(_split_specs(tm, d, npt)) if split_x else [pl.BlockSpec((tm, d), lambda i: (i, 0))]
    x_args = list(x) if split_x else [x]
    return pl.pallas_call(
        functools.partial(_proj_res_kernel, npt, split_x),
        out_shape=jax.ShapeDtypeStruct((t, d), F32),
        grid=(t // tm,),
        in_specs=x_specs + [
            *_split_specs(tm, k, npt),
            pl.BlockSpec((k, d), lambda i: (0, 0)),
            pl.BlockSpec((1, d), lambda i: (0, 0)),
        ],
        out_specs=pl.BlockSpec((tm, d), lambda i: (i, 0)),
        scratch_shapes=[pltpu.VMEM((k, d), BF16)],
        compiler_params=_params(("arbitrary",)),
        name="proj_residual",
    )(*x_args, o_prompt, o_sample, w, b.reshape(1, d))


def _split3_dot(a, w_hi, w_lo):
    a_hi = a.astype(BF16)
    a_lo = (a - a_hi.astype(F32)).astype(BF16)
    return _dot(a_hi, w_hi) + (_dot(a_hi, w_lo) + _dot(a_lo, w_hi))


def _router_kernel(x_ref, g_ref, w_ref, b_ref, f_ref, meta_ref, cnt_ref, whi_ref, wlo_ref, carry_ref):
    i = pl.program_id(0)
    tm = x_ref.shape[0]

    @pl.when(i == 0)
    def _():
        w = w_ref[...]
        hi = w.astype(BF16)
        whi_ref[...] = hi
        wlo_ref[...] = (w - hi.astype(F32)).astype(BF16)
        carry_ref[...] = jnp.zeros_like(carry_ref)

    f = _rms(x_ref[...], g_ref[...])
    f_ref[...] = f
    logits = _split3_dot(f, whi_ref[...], wlo_ref[...]) + b_ref[...]
    lane = lax.broadcasted_iota(I32, (tm, LANES), 1)
    lanef = lane.astype(F32)
    big = float(LANES)

    def first_max(mask):
        v = jnp.max(jnp.where(mask, logits, NEG_INF), axis=-1, keepdims=True)
        idx = jnp.min(jnp.where(mask & (logits == v), lanef, big), axis=-1, keepdims=True)
        return v, idx

    gmask = lane < N_GROUPS
    gmax, gidx = first_max(gmask)
    gsum = jnp.sum(jnp.where(gmask, jnp.exp(logits - gmax), 0.0), axis=-1, keepdims=True)
    g_gate = 1.0 / gsum
    lo = N_GROUPS + EXPERTS_PER_GROUP * gidx
    emask = (lanef >= lo) & (lanef < lo + EXPERTS_PER_GROUP)
    e1, i1 = first_max(emask)
    e2, i2 = first_max(emask & (lanef != i1))
    z = jnp.exp(e2 - e1)
    gate1 = g_gate / (1.0 + z)
    gate2 = g_gate * z / (1.0 + z)
    id1 = i1 - N_GROUPS
    id2 = i2 - N_GROUPS
    oh1 = (lanef == id1).astype(F32)
    oh2 = (lanef == id2).astype(F32)
    oh = oh1 + oh2
    rr = lax.broadcasted_iota(I32, (tm, tm), 0)
    cc = lax.broadcasted_iota(I32, (tm, tm), 1)
    tri = (cc < rr).astype(BF16)
    before = _dot(tri, oh.astype(BF16)) + carry_ref[...]
    rank1 = jnp.sum(oh1 * before, axis=-1, keepdims=True)
    rank2 = jnp.sum(oh2 * before, axis=-1, keepdims=True)
    carry_ref[...] = carry_ref[...] + jnp.sum(oh, axis=0, keepdims=True)
    cnt_ref[...] = carry_ref[...]
    meta = jnp.zeros((tm, LANES), F32)
    for k, v in enumerate((id1, id2, gate1, gate2, rank1, rank2)):
        meta = jnp.where(lane == k, v, meta)
    meta_ref[...] = meta


def moe_router(x, g, w_group, b_group, w_expert, b_expert):
    t, d = x.shape
    tm = _tile(t, 512)
    w = jnp.zeros((d, LANES), F32).at[:, :N_GROUPS].set(w_group).at[:, N_GROUPS:N_GROUPS + N_EXPERTS].set(w_expert)
    b = jnp.zeros((1, LANES), F32).at[0, :N_GROUPS].set(b_group).at[0, N_GROUPS:N_GROUPS + N_EXPERTS].set(b_expert)
    return pl.pallas_call(
        _router_kernel,
        out_shape=(
            jax.ShapeDtypeStruct((t, d), F32),
            jax.ShapeDtypeStruct((t, LANES), F32),
            jax.ShapeDtypeStruct((1, LANES), F32),
        ),
        grid=(t // tm,),
        in_specs=[
            pl.BlockSpec((tm, d), lambda i: (i, 0)),
            pl.BlockSpec((1, d), lambda i: (0, 0)),
            pl.BlockSpec((d, LANES), lambda i: (0, 0)),
            pl.BlockSpec((1, LANES), lambda i: (0, 0)),
        ],
        out_specs=(
            pl.BlockSpec((tm, d), lambda i: (i, 0)),
            pl.BlockSpec((tm, LANES), lambda i: (i, 0)),
            pl.BlockSpec((1, LANES), lambda i: (0, 0)),
        ),
        scratch_shapes=[pltpu.VMEM((d, LANES), BF16), pltpu.VMEM((d, LANES), BF16), pltpu.VMEM((1, LANES), F32)],
        compiler_params=_params(("arbitrary",)),
        name="moe_router",
    )(x, g.reshape(1, d), w, b)


def _dest_kernel(meta_ref, ps_ref, d_ref):
    meta = meta_ref[...]
    tm = meta.shape[0]
    lane = lax.broadcasted_iota(I32, (tm, LANES), 1)
    lanef = lane.astype(F32)
    ps = ps_ref[...]
    both = jnp.zeros((tm, LANES), F32)
    for k in range(TOP_K):
        start = jnp.sum(jnp.where(lanef == meta[:, k:k + 1], ps, 0.0), axis=-1, keepdims=True)
        both = jnp.where(lane == k, start + meta[:, 4 + k:5 + k], both)
    d_ref[0] = both.T[:8, :].astype(I32)


def moe_dest(meta, pstart):
    t = meta.shape[0]
    tm = _tile(t, COMBINE_TILE)
    return pl.pallas_call(
        _dest_kernel,
        out_shape=jax.ShapeDtypeStruct((t // tm, 8, tm), I32),
        grid=(t // tm,),
        in_specs=[pl.BlockSpec((tm, LANES), lambda i: (i, 0)), pl.BlockSpec((1, LANES), lambda i: (0, 0))],
        out_specs=pl.BlockSpec((1, 8, tm), lambda i: (i, 0, 0)),
        compiler_params=_params(("parallel",)),
        name="moe_dest",
    )(meta, pstart)


def _dispatch_kernel(zblk_ref, dcur_ref, dprev_ref, f_hbm, xs_hbm, zero_buf, sem, zsem):
    i = pl.program_id(0)
    n = pl.num_programs(0)
    tm = dcur_ref.shape[2]
    blk = zero_buf.shape[0]

    nblocks = xs_hbm.shape[0] // blk
    nused = zblk_ref[N_EXPERTS]

    def zero_copy(b):
        return pltpu.make_async_copy(zero_buf, xs_hbm.at[pl.ds(b * blk, blk)], zsem)

    @pl.when(i == 0)
    def _():
        zero_buf[...] = jnp.zeros_like(zero_buf)

        def zero_blocks(wait):
            def last_of_expert(e, carry):
                cp = zero_copy(zblk_ref[e])
                cp.wait() if wait else cp.start()
                return carry

            def unused(b, carry):
                cp = zero_copy(b)
                cp.wait() if wait else cp.start()
                return carry

            lax.fori_loop(0, N_EXPERTS, last_of_expert, 0)
            lax.fori_loop(nused, nblocks, unused, 0)

        zero_blocks(False)
        zero_blocks(True)

    def scatter(d_ref, tile, slot, wait):
        def body(r, carry):
            for k in range(TOP_K):
                cp = pltpu.make_async_copy(f_hbm.at[pl.ds(tile * tm + r, 1)], xs_hbm.at[pl.ds(d_ref[0, k, r], 1)],
                                           sem.at[slot])
                cp.wait() if wait else cp.start()
            return carry
        lax.fori_loop(0, tm, body, 0, unroll=8)

    scatter(dcur_ref, i, i % 2, False)

    @pl.when(i > 0)
    def _():
        scatter(dprev_ref, i - 1, (i - 1) % 2, True)

    @pl.when(i == n - 1)
    def _():
        scatter(dcur_ref, i, i % 2, True)


def moe_dispatch(f, dest, zblk, nslot):
    t, d = f.shape
    nt, _, tm = dest.shape
    grid_spec = pltpu.PrefetchScalarGridSpec(
        num_scalar_prefetch=1,
        grid=(nt,),
        in_specs=[
            pl.BlockSpec((1, 8, tm), lambda i, zb: (i, 0, 0), memory_space=pltpu.SMEM),
            pl.BlockSpec((1, 8, tm), lambda i, zb: (jnp.maximum(i - 1, 0), 0, 0), memory_space=pltpu.SMEM),
            pl.BlockSpec(memory_space=pl.ANY),
        ],
        out_specs=pl.BlockSpec(memory_space=pl.ANY),
        scratch_shapes=[pltpu.VMEM((MOE_BLOCK, d), F32), pltpu.SemaphoreType.DMA((2,)), pltpu.SemaphoreType.DMA],
    )
    return pl.pallas_call(
        _dispatch_kernel,
        out_shape=jax.ShapeDtypeStruct((nslot, d), F32),
        grid_spec=grid_spec,
        compiler_params=_params(("arbitrary",)),
        name="moe_dispatch",
    )(zblk, dest, dest, f)


def _expert_kernel(blk_e_ref, nblk_ref, x_ref, wg_ref, wu_ref, wd_ref, y_ref, wg_bf, wu_bf, wd_bf):
    j = pl.program_id(0)
    nb = nblk_ref[0]

    @pl.when(j < nb)
    def _():
        e = blk_e_ref[j]
        e_prev = blk_e_ref[jnp.maximum(j - 1, 0)]

        @pl.when(jnp.logical_or(j == 0, e != e_prev))
        def _():
            wg_bf[...] = wg_ref[...].astype(BF16)
            wu_bf[...] = wu_ref[...].astype(BF16)
            wd_bf[...] = wd_ref[...].astype(BF16)

        x = x_ref[...].astype(BF16)
        a = _dot(x, wg_bf[...])
        u = _dot(x, wu_bf[...])
        h = a * (1.0 / (1.0 + jnp.exp(-a))) * u
        y_ref[...] = _dot(h.astype(BF16), wd_bf[...])

    @pl.when(j >= nb)
    def _():
        y_ref[...] = jnp.zeros_like(y_ref)


def moe_experts(xs, blk_e, nblk, layer, w_gate, w_up, w_down):
    nslot, d = xs.shape
    nblocks = nslot // MOE_BLOCK
    de = w_gate.shape[3]
    wmap = lambda j, be, nb: (layer, be[j], 0, 0)
    grid_spec = pltpu.PrefetchScalarGridSpec(
        num_scalar_prefetch=2,
        grid=(nblocks,),
        in_specs=[
            pl.BlockSpec((MOE_BLOCK, d), lambda j, be, nb: (jnp.minimum(j, nb[0] - 1), 0)),
            pl.BlockSpec((None, None, d, de), wmap),
            pl.BlockSpec((None, None, d, de), wmap),
            pl.BlockSpec((None, None, de, d), wmap),
        ],
        out_specs=pl.BlockSpec((MOE_BLOCK, d), lambda j, be, nb: (j, 0)),
        scratch_shapes=[pltpu.VMEM((d, de), BF16), pltpu.VMEM((d, de), BF16), pltpu.VMEM((de, d), BF16)],
    )
    return pl.pallas_call(
        _expert_kernel,
        out_shape=jax.ShapeDtypeStruct((nslot, d), F32),
        grid_spec=grid_spec,
        compiler_params=_params(("arbitrary",)),
        name="moe_experts",
    )(blk_e, nblk, xs, w_gate, w_up, w_down)


def _combine_kernel(n_prompt_tiles, dcur_ref, dnxt_ref, yb_hbm, x_ref, meta_ref, g_ref, *refs):
    final_norm = n_prompt_tiles is not None
    if final_norm:
        op_ref, os_ref, ybuf, sem = refs
    else:
        o_ref, ybuf, sem = refs
    i = pl.program_id(0)
    n = pl.num_programs(0)
    tm = x_ref.shape[0]

    def gather(d_ref, slot, wait):
        def body(r, carry):
            for k in range(TOP_K):
                cp = pltpu.make_async_copy(yb_hbm.at[pl.ds(d_ref[0, k, r], 1)], ybuf.at[slot, k, pl.ds(r, 1)],
                                           sem.at[slot])
                cp.wait() if wait else cp.start()
            return carry
        lax.fori_loop(0, tm, body, 0, unroll=8)

    @pl.when(i == 0)
    def _():
        gather(dcur_ref, 0, False)

    @pl.when(i + 1 < n)
    def _():
        gather(dnxt_ref, (i + 1) % 2, False)

    slot = i % 2
    gather(dcur_ref, slot, True)
    meta = meta_ref[...]
    gate1 = meta[:, 2:3]
    gate2 = meta[:, 3:4]
    out = x_ref[...] + (ybuf[slot, 0] * gate1 + ybuf[slot, 1] * gate2)
    if final_norm:
        out = _rms(out, g_ref[...])

        @pl.when(i < n_prompt_tiles)
        def _():
            op_ref[...] = out

        @pl.when(i >= n_prompt_tiles)
        def _():
            os_ref[...] = out
    else:
        o_ref[...] = out


def moe_combine(x, yb, dest, meta, g_final, n_prompt):
    t, d = x.shape
    nt, _, tm = dest.shape
    cur = pl.BlockSpec((1, 8, tm), lambda i: (i, 0, 0), memory_space=pltpu.SMEM)
    nxt = pl.BlockSpec((1, 8, tm), lambda i: (jnp.minimum(i + 1, nt - 1), 0, 0), memory_space=pltpu.SMEM)
    if n_prompt is None:
        npt = None
        out_shape = jax.ShapeDtypeStruct((t, d), F32)
        out_specs = pl.BlockSpec((tm, d), lambda i: (i, 0))
    else:
        assert n_prompt % tm == 0
        npt = n_prompt // tm
        out_shape = (jax.ShapeDtypeStruct((n_prompt, d), F32), jax.ShapeDtypeStruct((t - n_prompt, d), F32))
        out_specs = _split_specs(tm, d, npt)
    return pl.pallas_call(
        functools.partial(_combine_kernel, npt),
        out_shape=out_shape,
        grid=(nt,),
        in_specs=[
            cur, nxt,
            pl.BlockSpec(memory_space=pl.ANY),
            pl.BlockSpec((tm, d), lambda i: (i, 0)),
            pl.BlockSpec((tm, LANES), lambda i: (i, 0)),
            pl.BlockSpec((1, d), lambda i: (0, 0)),
        ],
        out_specs=out_specs,
        scratch_shapes=[pltpu.VMEM((2, TOP_K, tm, d), F32), pltpu.SemaphoreType.DMA((2,))],
        compiler_params=_params(("arbitrary",)),
        name="moe_combine",
    )(dest, dest, yb, x, meta, g_final.reshape(1, d))


def hier_moe_layer(x, layer, norm_g, w_group, b_group, w_expert, b_expert, w_gate, w_up, w_down, g_final,
                   n_prompt):
    t, d = x.shape
    f, meta, counts = moe_router(x, norm_g, w_group, b_group, w_expert, b_expert)
    cnt = counts[0, :N_EXPERTS].astype(I32)
    padded = (cnt + MOE_BLOCK - 1) // MOE_BLOCK * MOE_BLOCK
    pend = jnp.cumsum(padded)
    pstart = pend - padded
    nblocks = -(-(t * TOP_K) // MOE_BLOCK) + N_EXPERTS
    blk_start = jnp.arange(nblocks, dtype=I32) * MOE_BLOCK
    blk_e = jnp.minimum(jnp.sum((pend[None, :] <= blk_start[:, None]).astype(I32), axis=1), N_EXPERTS - 1)
    nblk = (pend[-1:] // MOE_BLOCK).astype(I32)
    zblk = jnp.concatenate([jnp.maximum(pend // MOE_BLOCK - 1, 0), pend[-1:] // MOE_BLOCK]).astype(I32)
    ps_row = jnp.concatenate([pstart.astype(F32), jnp.zeros((LANES - N_EXPERTS,), F32)]).reshape(1, LANES)
    dest = moe_dest(meta, ps_row)
    xs = moe_dispatch(f, dest, zblk, nblocks * MOE_BLOCK)
    yb = moe_experts(xs, blk_e, nblk, layer, w_gate, w_up, w_down)
    return moe_combine(x, yb, dest, meta, g_final, n_prompt)


def _mla_down_kernel(x_ref, g_ref, w_ref, gq_ref, gkv_ref, cos_ref, sin_ref, cq_ref, ckr_ref, wbf_ref):
    @pl.when(pl.program_id(0) == 0)
    def _():
        wbf_ref[...] = w_ref[...].astype(BF16)

    h = _rms(x_ref[...], g_ref[...])
    a = _dot(h.astype(BF16), wbf_ref[...])
    q0, c0 = MLA_Q_LORA, MLA_Q_LORA + MLA_KV_LORA
    cq_ref[...] = _rms(a[:, :q0], gq_ref[...]).astype(BF16)
    ckr_ref[:, :MLA_KV_LORA] = _rms(a[:, q0:c0], gkv_ref[...])
    ckr_ref[:, MLA_KV_LORA:] = a[:, c0:c0 + LANES] * cos_ref[...] + a[:, c0 + LANES:] * sin_ref[...]


def _rot_cols(w):
    half = MLA_ROPE // 2
    return jnp.concatenate([-w[..., half:], w[..., :half]], axis=-1)


def mla_down(x, g, w_dqkv, norm_q, norm_kv, cos_t, sin_t):
    t, d = x.shape
    tm = _tile(t, 512)
    q0, c0 = MLA_Q_LORA, MLA_Q_LORA + MLA_KV_LORA
    w_r = w_dqkv[:, c0:]
    zpad = jnp.zeros((d, LANES - MLA_ROPE), F32)
    w = jnp.concatenate([w_dqkv[:, :c0], w_r, zpad, _rot_cols(w_r), zpad], axis=1)
    n = w.shape[1]
    return pl.pallas_call(
        _mla_down_kernel,
        out_shape=(jax.ShapeDtypeStruct((t, q0), BF16), jax.ShapeDtypeStruct((t, MLA_KV_LORA + LANES), F32)),
        grid=(t // tm,),
        in_specs=[
            pl.BlockSpec((tm, d), lambda i: (i, 0)),
            pl.BlockSpec((1, d), lambda i: (0, 0)),
            pl.BlockSpec((d, n), lambda i: (0, 0)),
            pl.BlockSpec((1, q0), lambda i: (0, 0)),
            pl.BlockSpec((1, MLA_KV_LORA), lambda i: (0, 0)),
            pl.BlockSpec((tm, LANES), lambda i: (i, 0)),
            pl.BlockSpec((tm, LANES), lambda i: (i, 0)),
        ],
        out_specs=(pl.BlockSpec((tm, q0), lambda i: (i, 0)),
                   pl.BlockSpec((tm, MLA_KV_LORA + LANES), lambda i: (i, 0))),
        scratch_shapes=[pltpu.VMEM((d, n), BF16)],
        compiler_params=_params(("arbitrary",)),
        name="mla_down",
    )(x, g.reshape(1, d), w, norm_q.reshape(1, q0), norm_kv.reshape(1, MLA_KV_LORA), cos_t, sin_t)


def _mla_up_kernel(cq_ref, ckr_ref, qc_ref, qs_ref, wq_ref, wk_ref, wv_ref, q_ref, k_ref, v_ref,
                   wq_bf, wk_bf, wv_bf):
    @pl.when(pl.program_id(0) == 0)
    def _():
        wq_bf[...] = wq_ref[...].astype(BF16)
        wk_bf[...] = wk_ref[...].astype(BF16)
        wv_bf[...] = wv_ref[...].astype(BF16)

    nq = q_ref.shape[1]
    ab = _dot(cq_ref[...], wq_bf[...])
    qc = qc_ref[...]
    qs = qs_ref[...]
    for h in range(MLA_HEADS):
        sl = slice(h * LANES, (h + 1) * LANES)
        sl_b = slice(nq + h * LANES, nq + (h + 1) * LANES)
        q_ref[:, sl] = (ab[:, sl] * qc + ab[:, sl_b] * qs).astype(BF16)
    ckr = ckr_ref[...].astype(BF16)
    k_ref[...] = _dot(ckr, wk_bf[...]).astype(BF16)
    v_ref[...] = _dot(ckr[:, :MLA_KV_LORA], wv_bf[...]).astype(BF16)


def _head_slots(w, lo):
    r, h, n = w.shape
    out = jnp.zeros((r, h, LANES), F32).at[:, :, lo:lo + n].set(w)
    return out.reshape(r, h * LANES)


def mla_up(cq, ckr, q_cos, q_sin, w_uq, w_uk, w_uv):
    t = cq.shape[0]
    tm = _tile(t, 256)
    nq = MLA_HEADS * LANES
    kin = MLA_KV_LORA + LANES
    wq3 = w_uq.reshape(MLA_Q_LORA, MLA_HEADS, MLA_NOPE + MLA_ROPE)
    wq_a = _head_slots(wq3, 0)
    wq_b = _head_slots(_rot_cols(wq3[:, :, MLA_NOPE:]), MLA_NOPE)
    wq = jnp.concatenate([wq_a, wq_b], axis=1)
    place = jnp.zeros((LANES, MLA_HEADS, LANES), F32)
    place = place.at[jnp.arange(MLA_ROPE), :, MLA_NOPE + jnp.arange(MLA_ROPE)].set(1.0)
    wk = jnp.concatenate([_head_slots(w_uk, 0), place.reshape(LANES, nq)], axis=0)
    even = (jnp.arange(MLA_HEADS) % 2 == 0)[None, :, None]
    wv = jnp.where(even, _head_slots(w_uv, 0).reshape(MLA_KV_LORA, MLA_HEADS, LANES),
                   _head_slots(w_uv, MLA_V).reshape(MLA_KV_LORA, MLA_HEADS, LANES)).reshape(MLA_KV_LORA, nq)
    full = lambda r, c: pl.BlockSpec((r, c), lambda i: (0, 0))
    rows = lambda c: pl.BlockSpec((tm, c), lambda i: (i, 0))
    return pl.pallas_call(
        _mla_up_kernel,
        out_shape=tuple(jax.ShapeDtypeStruct((t, nq), BF16) for _ in range(3)),
        grid=(t // tm,),
        in_specs=[rows(MLA_Q_LORA), rows(kin), rows(LANES), rows(LANES),
                  full(MLA_Q_LORA, 2 * nq), full(kin, nq), full(MLA_KV_LORA, nq)],
        out_specs=(rows(nq), rows(nq), rows(nq)),
        scratch_shapes=[pltpu.VMEM((MLA_Q_LORA, 2 * nq), BF16), pltpu.VMEM((kin, nq), BF16),
                        pltpu.VMEM((MLA_KV_LORA, nq), BF16)],
        compiler_params=_params(("arbitrary",)),
        name="mla_up",
    )(cq, ckr, q_cos, q_sin, wq, wk, wv)


def _mla_flash_kernel(q_ref, k_ref, v_ref, o_ref, m_ref, l_ref, acc_ref):
    qi = pl.program_id(2)
    tq = q_ref.shape[0]
    lane = lax.broadcasted_iota(I32, (tq, LANES), 1)
    first_half = lane < MLA_V
    m_ref[...] = jnp.full_like(m_ref, NEG_INF)
    l_ref[...] = jnp.zeros_like(l_ref)
    acc_ref[...] = jnp.zeros_like(acc_ref)
    q = q_ref[...]

    def step(kb, masked):
        start = pl.multiple_of(kb * tq, tq)
        kblk = k_ref[pl.ds(start, tq), :]
        vblk = v_ref[pl.ds(start, tq), :]
        pv = None
        alphas = []
        for h in range(2):
            s = _dot_nt(q[:, h * LANES:(h + 1) * LANES], kblk[:, h * LANES:(h + 1) * LANES])
            if masked:
                row = lax.broadcasted_iota(I32, s.shape, 0)
                col = lax.broadcasted_iota(I32, s.shape, 1)
                s = jnp.where(col <= row, s, NEG_INF)
            sc = [s[:, c * LANES:(c + 1) * LANES] for c in range(tq // LANES)]
            m_old = m_ref[h]
            m_new = jnp.maximum(m_old, jnp.max(functools.reduce(jnp.maximum, sc), axis=-1, keepdims=True))
            alpha = jnp.exp2(m_old - m_new)
            pc = [jnp.exp2(x - m_new) for x in sc]
            l_ref[h] = alpha * l_ref[h] + functools.reduce(jnp.add, pc)
            m_ref[h] = m_new
            p = jnp.concatenate([x.astype(BF16) for x in pc], axis=1)
            part = _dot(p, vblk[:, h * LANES:(h + 1) * LANES])
            pv = part if pv is None else pv + part
            alphas.append(alpha)
        acc_ref[...] = jnp.where(first_half, alphas[0], alphas[1]) * acc_ref[...] + pv

    def body(kb, carry):
        step(kb, False)
        return carry

    lax.fori_loop(0, qi, body, 0)
    step(qi, True)
    l0 = jnp.sum(l_ref[0], axis=-1, keepdims=True)
    l1 = jnp.sum(l_ref[1], axis=-1, keepdims=True)
    o_ref[...] = (acc_ref[...] * jnp.where(first_half, 1.0 / l0, 1.0 / l1)).astype(BF16)


def mla_prompt_attention(qp, kp, vp, batch, seq):
    tq = _tile(seq, MLA_TQ)
    nqb = seq // tq
    npair = MLA_HEADS // 2
    pw = 2 * LANES
    return pl.pallas_call(
        _mla_flash_kernel,
        out_shape=jax.ShapeDtypeStruct((batch * seq, MLA_HEADS * MLA_V), BF16),
        grid=(batch, npair, nqb),
        in_specs=[
            pl.BlockSpec((tq, pw), lambda b, h, i: (b * nqb + i, h)),
            pl.BlockSpec((seq, pw), lambda b, h, i: (b, h)),
            pl.BlockSpec((seq, pw), lambda b, h, i: (b, h)),
        ],
        out_specs=pl.BlockSpec((tq, LANES), lambda b, h, i: (b * nqb + i, h)),
        scratch_shapes=[pltpu.VMEM((2, tq, LANES), F32), pltpu.VMEM((2, tq, LANES), F32),
                        pltpu.VMEM((tq, LANES), F32)],
        compiler_params=_params(("parallel", "parallel", "arbitrary")),
        name="mla_prompt",
    )(qp, kp, vp)


def _mla_absorb_kernel(q_ref, w_ref, o_ref):
    o_ref[0] = _dot(q_ref[...], w_ref[0].astype(BF16)).astype(BF16)


def mla_absorb(q_s, w_uk):
    ns = q_s.shape[0]
    kin = MLA_KV_LORA + LANES
    w = jnp.zeros((MLA_HEADS, LANES, kin), F32)
    w = w.at[:, :MLA_NOPE, :MLA_KV_LORA].set(jnp.transpose(w_uk, (1, 2, 0)))
    w = w.at[:, MLA_NOPE + jnp.arange(MLA_ROPE), MLA_KV_LORA + jnp.arange(MLA_ROPE)].set(1.0)
    return pl.pallas_call(
        _mla_absorb_kernel,
        out_shape=jax.ShapeDtypeStruct((MLA_HEADS, ns, kin), BF16),
        grid=(MLA_HEADS,),
        in_specs=[pl.BlockSpec((ns, LANES), lambda h: (0, h)),
                  pl.BlockSpec((1, LANES, kin), lambda h: (h, 0, 0))],
        out_specs=pl.BlockSpec((1, ns, kin), lambda h: (h, 0, 0)),
        compiler_params=_params(("parallel",)),
        name="mla_absorb",
    )(q_s, w)


def _mla_decode_kernel(npg, dec_t, pt_ref, q_ref, new_ref, lat_hbm, krt_hbm, o_ref,
                       lat_buf, krt_buf, sem, s_buf, latb_buf):
    b = pl.program_id(0)
    nb = pl.num_programs(0)
    npages = pt_ref.shape[1]
    page = lat_hbm.shape[1]
    ck = npg * page
    nchunk = npages // npg
    nrow = q_ref.shape[0]

    def page_copies(bb, slot, i):
        pg = pt_ref[bb, i]
        return (pltpu.make_async_copy(lat_hbm.at[pg], lat_buf.at[slot, pl.ds(i * page, page)], sem.at[slot]),
                pltpu.make_async_copy(krt_hbm.at[pg], krt_buf.at[slot, pl.ds(i * MLA_ROPE, MLA_ROPE)],
                                      sem.at[slot]))

    def fetch(bb, slot, wait):
        def body(i, carry):
            for cp in page_copies(bb, slot, i):
                cp.wait() if wait else cp.start()
            return carry
        lax.fori_loop(0, npages, body, 0, unroll=8)

    @pl.when(b == 0)
    def _():
        fetch(0, 0, False)

    @pl.when(b + 1 < nb)
    def _():
        fetch(b + 1, (b + 1) % 2, False)

    slot = b % 2
    fetch(b, slot, True)

    q = q_ref[...]
    ql = q[:, :MLA_KV_LORA]
    qr = q[:, MLA_KV_LORA:MLA_KV_LORA + MLA_ROPE]
    mloc = jnp.full((nrow, LANES), NEG_INF, F32)
    for c in range(nchunk):
        lat = lat_buf[slot, c * ck:(c + 1) * ck, :].astype(BF16)
        latb_buf[c * ck:(c + 1) * ck, :] = lat
        krt = jnp.concatenate(
            [krt_buf[slot, (c * npg + i) * MLA_ROPE:(c * npg + i + 1) * MLA_ROPE, :] for i in range(npg)],
            axis=1).astype(BF16)
        s = _dot_nt(ql, lat) + _dot(qr, krt)
        s_buf[:, c * ck:(c + 1) * ck] = s
        for j in range(ck // LANES):
            mloc = jnp.maximum(mloc, s[:, j * LANES:(j + 1) * LANES])

    new = new_ref[...].astype(BF16)
    sn = _dot_nt(q, new)
    t_row = _idiv(lax.broadcasted_iota(I32, sn.shape, 0), MLA_HEADS)
    t_col = lax.broadcasted_iota(I32, sn.shape, 1)
    sn = jnp.where(t_col <= t_row, sn, NEG_INF)
    m = jnp.maximum(jnp.max(mloc, axis=-1, keepdims=True), jnp.max(sn, axis=-1, keepdims=True))
    pn = jnp.exp2(sn - m)
    lsum = jnp.zeros((nrow, LANES), F32)
    acc = jnp.zeros((nrow, MLA_KV_LORA), F32)
    for c in range(nchunk):
        pc = [jnp.exp2(s_buf[:, c * ck + j * LANES:c * ck + (j + 1) * LANES] - m) for j in range(ck // LANES)]
        lsum = lsum + functools.reduce(jnp.add, pc)
        p = jnp.concatenate([x.astype(BF16) for x in pc], axis=1)
        acc = acc + _dot(p, latb_buf[c * ck:(c + 1) * ck, :])
    l = jnp.sum(lsum, axis=-1, keepdims=True) + jnp.sum(pn, axis=-1, keepdims=True)
    pnb = pn.astype(BF16).astype(F32)
    cn = new[:, :MLA_KV_LORA].astype(F32)
    for t in range(dec_t):
        acc = acc + pnb[:, t:t + 1] * cn[t:t + 1, :]
    o_ref[...] = (acc / l).astype(BF16)


def mla_decode(qabs, ckr_new, cache_lat, cache_krt, page_table, dec_t):
    nbatch, npages = page_table.shape
    npg = _tile(npages, MLA_PAGES_PER_STEP)
    nrow = dec_t * MLA_HEADS
    kin = MLA_KV_LORA + LANES
    page = cache_lat.shape[1]
    nkeys = npages * page
    grid_spec = pltpu.PrefetchScalarGridSpec(
        num_scalar_prefetch=1,
        grid=(nbatch,),
        in_specs=[pl.BlockSpec((nrow, kin), lambda b, pt: (b, 0)),
                  pl.BlockSpec((None, 8, kin), lambda b, pt: (b, 0, 0)),
                  pl.BlockSpec(memory_space=pl.ANY),
                  pl.BlockSpec(memory_space=pl.ANY)],
        out_specs=pl.BlockSpec((nrow, MLA_KV_LORA), lambda b, pt: (b, 0)),
        scratch_shapes=[pltpu.VMEM((2, nkeys, MLA_KV_LORA), F32),
                        pltpu.VMEM((2, npages * MLA_ROPE, page), F32),
                        pltpu.SemaphoreType.DMA((2,)),
                        pltpu.VMEM((nrow, nkeys), F32),
                        pltpu.VMEM((nkeys, MLA_KV_LORA), BF16)],
    )
    return pl.pallas_call(
        functools.partial(_mla_decode_kernel, npg, dec_t),
        out_shape=jax.ShapeDtypeStruct((nbatch * nrow, MLA_KV_LORA), BF16),
        grid_spec=grid_spec,
        compiler_params=_params(("arbitrary",)),
        name="mla_decode",
    )(page_table, qabs, ckr_new, cache_lat, cache_krt)


def _mla_unabsorb_kernel(ol_ref, w_ref, o_ref):
    r = _dot(ol_ref[...], w_ref[...].astype(BF16))
    row_h = _imod(lax.broadcasted_iota(I32, r.shape, 0), MLA_HEADS)
    col_h = _idiv(lax.broadcasted_iota(I32, r.shape, 1), MLA_V)
    r = jnp.where(row_h == col_h, r, 0.0)
    o_ref[...] = jnp.sum(r.reshape(r.shape[0] // MLA_HEADS, MLA_HEADS, r.shape[1]), axis=1).astype(BF16)


def mla_unabsorb(o_lat, w_uv):
    rows = o_lat.shape[0]
    tr = _tile(rows, 1024)
    n = MLA_HEADS * MLA_V
    return pl.pallas_call(
        _mla_unabsorb_kernel,
        out_shape=jax.ShapeDtypeStruct((rows // MLA_HEADS, n), BF16),
        grid=(rows // tr,),
        in_specs=[pl.BlockSpec((tr, MLA_KV_LORA), lambda i: (i, 0)),
                  pl.BlockSpec((MLA_KV_LORA, n), lambda i: (0, 0))],
        out_specs=pl.BlockSpec((tr // MLA_HEADS, n), lambda i: (i, 0)),
        compiler_params=_params(("parallel",)),
        name="mla_unabsorb",
    )(o_lat, w_uv.reshape(MLA_KV_LORA, n))


def _rope_tables(pos):
    half = MLA_ROPE // 2
    inv = jnp.power(ROPE_THETA, -jnp.arange(half, dtype=F32) * 2.0 / MLA_ROPE)
    ang = pos[:, None] * inv[None, :]
    cos2 = jnp.tile(jnp.cos(ang), (1, 2))
    sin2 = jnp.tile(jnp.sin(ang), (1, 2))
    t = pos.shape[0]
    scale = (MLA_NOPE + MLA_ROPE) ** -0.5 * LOG2E
    zeros = lambda n: jnp.zeros((t, n), F32)
    k_cos = jnp.concatenate([cos2, zeros(LANES - MLA_ROPE)], axis=1)
    k_sin = jnp.concatenate([sin2, zeros(LANES - MLA_ROPE)], axis=1)
    tail = LANES - MLA_NOPE - MLA_ROPE
    q_cos = jnp.concatenate([jnp.full((t, MLA_NOPE), scale, F32), scale * cos2, zeros(tail)], axis=1)
    q_sin = jnp.concatenate([zeros(MLA_NOPE), scale * sin2, zeros(tail)], axis=1)
    return k_cos, k_sin, q_cos, q_sin


def kernel(x_prompt, x_sample, cache_swa_k, cache_swa_v, cache_mla_latent, cache_mla_krope, page_table,
           norm_attn, norm_ffn, norm_final,
           swa_w_qkv, swa_b_qkv, swa_sinks, swa_w_o, swa_b_o,
           mla_w_dqkv, mla_norm_q, mla_norm_kv, mla_w_uq, mla_w_uk, mla_w_uv, mla_w_o,
           moe_w_group, moe_b_group, moe_w_expert, moe_b_expert, moe_w_gate, moe_w_up, moe_w_down):
    batch, seq, d = x_prompt.shape
    nbatch, dec_t, _ = x_sample.shape
    n_p = batch * seq
    n_s = nbatch * dec_t
    npages = page_table.shape[1]
    page = cache_mla_latent.shape[2]
    past_len = npages * page
    nq = SWA_HEADS * SWA_HD
    kvw = SWA_KV * SWA_HD
    x_in = (x_prompt.reshape(n_p, d), x_sample.reshape(n_s, d))

    w_qkv = swa_w_qkv[0]
    w_q = w_qkv[:, :nq].reshape(d, SWA_KV, SWA_G, SWA_HD).transpose(0, 2, 1, 3).reshape(d, nq)
    b_q = swa_b_qkv[0][:nq].reshape(SWA_KV, SWA_G, SWA_HD).transpose(1, 0, 2).reshape(nq)
    w0 = jnp.concatenate([w_q, w_qkv[:, nq:]], axis=1)
    b0 = jnp.concatenate([b_q, swa_b_qkv[0][nq:]])
    w_o0 = swa_w_o[0].reshape(SWA_KV, SWA_G, SWA_HD, d).transpose(1, 0, 2, 3).reshape(nq, d)
    qkv = norm_proj(*x_in, norm_attn[0], w0, b0)
    o_p = swa_prompt_attention(qkv, swa_sinks[0], batch, seq)
    qkv_s = qkv[n_p:]
    q_s = qkv_s[:, :nq].reshape(nbatch, dec_t, SWA_G, kvw).transpose(0, 2, 1, 3).reshape(nbatch, SWA_G * dec_t, kvw)
    k_s = qkv_s[:, nq:nq + kvw].reshape(nbatch, dec_t, kvw)
    v_s = qkv_s[:, nq + kvw:].reshape(nbatch, dec_t, kvw)
    pad8 = lambda a: jnp.pad(a, ((0, 0), (0, 8 - dec_t), (0, 0)))
    ck = cache_swa_k[0].reshape(nbatch, WINDOW, kvw)
    cv = cache_swa_v[0].reshape(nbatch, WINDOW, kvw)
    o_s = swa_sample_attention(q_s, pad8(k_s), pad8(v_s), ck, cv, swa_sinks[0], dec_t)
    o_s = o_s.reshape(nbatch, SWA_G, dec_t, kvw).transpose(0, 2, 1, 3).reshape(n_s, nq)
    x = proj_residual(x_in, o_p, o_s, w_o0, swa_b_o[0])
    x = hier_moe_layer(x, 0, norm_ffn[0], moe_w_group[0], moe_b_group[0], moe_w_expert[0], moe_b_expert[0],
                       moe_w_gate, moe_w_up, moe_w_down, norm_final, None)

    last = jnp.stack([qkv[(b + 1) * seq - WINDOW:(b + 1) * seq, nq:] for b in range(batch)])
    swa_kp = last[:, :, :kvw].reshape(1, batch, WINDOW, SWA_KV, SWA_HD)
    swa_vp = last[:, :, kvw:].reshape(1, batch, WINDOW, SWA_KV, SWA_HD)
    swa_ks = jnp.concatenate([cache_swa_k[0], k_s.reshape(nbatch, dec_t, SWA_KV, SWA_HD)], axis=1)[:, dec_t:][None]
    swa_vs = jnp.concatenate([cache_swa_v[0], v_s.reshape(nbatch, dec_t, SWA_KV, SWA_HD)], axis=1)[:, dec_t:][None]

    pos = jnp.concatenate([jnp.tile(jnp.arange(seq, dtype=F32), batch),
                           jnp.tile(past_len + jnp.arange(dec_t, dtype=F32), nbatch)])
    k_cos, k_sin, q_cos, q_sin = _rope_tables(pos)
    cq, ckr = mla_down(x, norm_attn[1], mla_w_dqkv[0], mla_norm_q[0], mla_norm_kv[0], k_cos, k_sin)
    qp, kp, vp = mla_up(cq, ckr, q_cos, q_sin, mla_w_uq[0], mla_w_uk[0], mla_w_uv[0])
    o_p = mla_prompt_attention(qp, kp, vp, batch, seq)
    qabs = mla_absorb(qp[n_p:], mla_w_uk[0])
    qabs = qabs.transpose(1, 0, 2).reshape(n_s * MLA_HEADS, MLA_KV_LORA + LANES)
    ckr_new = jnp.pad(ckr[n_p:].reshape(nbatch, dec_t, MLA_KV_LORA + LANES), ((0, 0), (0, 8 - dec_t), (0, 0)))
    o_lat = mla_decode(qabs, ckr_new, cache_mla_latent[0], jnp.swapaxes(cache_mla_krope[0], 1, 2), page_table, dec_t)
    o_s = mla_unabsorb(o_lat, mla_w_uv[0])
    x = proj_residual(x, o_p, o_s, mla_w_o[0], jnp.zeros((d,), F32))
    y_p, y_s = hier_moe_layer(x, 1, norm_ffn[1], moe_w_group[1], moe_b_group[1], moe_w_expert[1], moe_b_expert[1],
                              moe_w_gate, moe_w_up, moe_w_down, norm_final, n_p)

    c_all = ckr[:, :MLA_KV_LORA]
    r_all = ckr[:, MLA_KV_LORA:MLA_KV_LORA + MLA_ROPE]
    return (y_p.reshape(batch, seq, d), y_s.reshape(nbatch, dec_t, d),
            swa_kp, swa_vp, swa_ks, swa_vs,
            c_all[:n_p].reshape(1, batch, seq, MLA_KV_LORA), r_all[:n_p].reshape(1, batch, seq, MLA_ROPE),
            c_all[n_p:].reshape(1, nbatch, dec_t, MLA_KV_LORA), r_all[n_p:].reshape(1, nbatch, dec_t, MLA_ROPE))
```

```python
import functools

import jax
import jax.numpy as jnp
import numpy as np
from jax import lax
from jax.experimental import pallas as pl
from jax.experimental.pallas import tpu as pltpu

F32 = jnp.float32
BF16 = jnp.bfloat16
I32 = jnp.int32

SWA_HEADS = 16
SWA_KV = 4
SWA_G = SWA_HEADS // SWA_KV
SWA_HD = 64
WINDOW = 128
MLA_HEADS = 16
MLA_Q_LORA = 384
MLA_KV_LORA = 256
MLA_NOPE = 64
MLA_ROPE = 32
MLA_V = 64
ROPE_THETA = 10000.0
N_GROUPS = 8
EXPERTS_PER_GROUP = 8
N_EXPERTS = N_GROUPS * EXPERTS_PER_GROUP
TOP_K = 2
RMS_EPS = 1e-6
NEG_INF = -1e30
LOG2E = 1.4426950408889634

LANES = 128
VMEM_LIMIT_BYTES = 52 * 1024 * 1024

MOE_BLOCK = 256
COMBINE_TILE = 256
MLA_TQ = 1024
MLA_PAGES_PER_STEP = 8
SWA_SAMPLE_SEQS = 8


def _tile(n, pref):
    t = pref
    while t > 8 and n % t:
        t //= 2
    assert n % t == 0, (n, pref)
    return t


def _params(sem, vmem=VMEM_LIMIT_BYTES):
    return pltpu.CompilerParams(dimension_semantics=sem, vmem_limit_bytes=vmem)


def _log2(n):
    assert n > 0 and n & (n - 1) == 0, n
    return n.bit_length() - 1


def _idiv(v, n):
    return lax.shift_right_logical(v, _log2(n))


def _imod(v, n):
    assert n & (n - 1) == 0, n
    return jnp.bitwise_and(v, n - 1)


def _rms(x, g):
    return x * lax.rsqrt(jnp.mean(x * x, axis=-1, keepdims=True) + RMS_EPS) * g


def _dot(a, b):
    return jnp.dot(a, b, preferred_element_type=F32)


def _dot_nt(a, b):
    return lax.dot_general(a, b, (((1,), (1,)), ((), ())), preferred_element_type=F32)


def _split_specs(tm, width, n_first_tiles):
    return (pl.BlockSpec((tm, width), lambda i: (jnp.minimum(i, n_first_tiles - 1), 0)),
            pl.BlockSpec((tm, width), lambda i: (jnp.maximum(i - n_first_tiles, 0), 0)))


def _norm_proj_kernel(n_prompt_tiles, xp_ref, xs_ref, g_ref, w_ref, b_ref, o_ref, wbf_ref):
    i = pl.program_id(0)

    @pl.when(i == 0)
    def _():
        wbf_ref[...] = w_ref[...].astype(BF16)

    x = jnp.where(i < n_prompt_tiles, xp_ref[...], xs_ref[...])
    h = _rms(x, g_ref[...])
    o_ref[...] = _dot(h.astype(BF16), wbf_ref[...]) + b_ref[...]


def norm_proj(x_p, x_s, g, w, b):
    d = x_p.shape[1]
    t = x_p.shape[0] + x_s.shape[0]
    n = w.shape[1]
    tm = _tile(x_s.shape[0], 512)
    assert x_p.shape[0] % tm == 0
    npt = x_p.shape[0] // tm
    return pl.pallas_call(
        functools.partial(_norm_proj_kernel, npt),
        out_shape=jax.ShapeDtypeStruct((t, n), F32),
        grid=(t // tm,),
        in_specs=[
            *_split_specs(tm, d, npt),
            pl.BlockSpec((1, d), lambda i: (0, 0)),
            pl.BlockSpec((d, n), lambda i: (0, 0)),
            pl.BlockSpec((1, n), lambda i: (0, 0)),
        ],
        out_specs=pl.BlockSpec((tm, n), lambda i: (i, 0)),
        scratch_shapes=[pltpu.VMEM((d, n), BF16)],
        compiler_params=_params(("arbitrary",)),
        name="norm_proj",
    )(x_p, x_s, g.reshape(1, d), w, b.reshape(1, n))


def _block_diag(x, nseg):
    w = x.shape[1] // nseg
    seg = _idiv(lax.broadcasted_iota(I32, x.shape, 1), w)
    zero = jnp.zeros_like(x)
    return jnp.concatenate([jnp.where(seg == s, x, zero) for s in range(nseg)], axis=0)


def _slope(kv, g):
    return 2.0 ** (-8.0 * (kv * SWA_G + g + 1) / SWA_HEADS)


def _swa_bias_tables():
    qi = np.arange(WINDOW)[:, None]
    kj = np.arange(2 * WINDOW)[None, :]
    dist = qi + WINDOW - kj
    band = (dist >= 0) & (dist < WINDOW)
    out = np.empty((2, SWA_G * SWA_KV, WINDOW, 2 * WINDOW), np.float32)
    for g in range(SWA_G):
        for kv in range(SWA_KV):
            bias = -_slope(kv, g) * dist * LOG2E
            out[1, g * SWA_KV + kv] = np.where(band, bias, NEG_INF)
            out[0, g * SWA_KV + kv] = np.where(band & (kj >= WINDOW), bias, NEG_INF)
    return out


def _swa_prompt_kernel(sink_ref, bias_ref, q_ref, kc_ref, kp_ref, vc_ref, vp_ref, o_ref):
    kvw = SWA_KV * SWA_HD
    k2 = jnp.concatenate([kp_ref[...], kc_ref[...]], axis=0).astype(BF16)
    v2 = jnp.concatenate([vp_ref[...], vc_ref[...]], axis=0).astype(BF16)
    kbd = _block_diag(k2, SWA_KV)
    vbd = _block_diag(v2, SWA_KV)
    qall = jnp.concatenate([q_ref[:, g * kvw:(g + 1) * kvw] for g in range(SWA_G)], axis=0)
    qall = (qall * (SWA_HD ** -0.5 * LOG2E)).astype(BF16)
    s = _dot_nt(qall, kbd)
    rows = []
    for g in range(SWA_G):
        ps = []
        for kv in range(SWA_KV):
            sink = sink_ref[kv * SWA_G + g] * LOG2E
            sc = s[g * WINDOW:(g + 1) * WINDOW, kv * 2 * WINDOW:(kv + 1) * 2 * WINDOW] + bias_ref[g * SWA_KV + kv]
            m = jnp.maximum(jnp.max(sc, axis=-1, keepdims=True), sink)
            p = jnp.exp2(sc - m)
            den = jnp.sum(p, axis=-1, keepdims=True) + jnp.exp2(sink - m)
            ps.append((p * (1.0 / den)).astype(BF16))
        rows.append(jnp.concatenate(ps, axis=1))
    og = _dot(jnp.concatenate(rows, axis=0), vbd)
    for g in range(SWA_G):
        o_ref[:, g * kvw:(g + 1) * kvw] = og[g * WINDOW:(g + 1) * WINDOW].astype(BF16)


def swa_prompt_attention(qkv, sinks, batch, seq):
    nb = seq // WINDOW
    nq = SWA_HEADS * SWA_HD
    kvw = SWA_KV * SWA_HD
    kcol = nq // kvw
    bias = jnp.asarray(_swa_bias_tables())
    cur = lambda b, n: (b * nb + n, kcol)
    prev = lambda b, n: (b * nb + jnp.maximum(n - 1, 0), kcol)
    cur_v = lambda b, n: (b * nb + n, kcol + 1)
    prev_v = lambda b, n: (b * nb + jnp.maximum(n - 1, 0), kcol + 1)
    return pl.pallas_call(
        _swa_prompt_kernel,
        out_shape=jax.ShapeDtypeStruct((batch * seq, nq), BF16),
        grid=(batch, nb),
        in_specs=[
            pl.BlockSpec(memory_space=pltpu.SMEM),
            pl.BlockSpec((None,) + bias.shape[1:], lambda b, n: (jnp.minimum(n, 1), 0, 0, 0)),
            pl.BlockSpec((WINDOW, nq), lambda b, n: (b * nb + n, 0)),
            pl.BlockSpec((WINDOW, kvw), cur),
            pl.BlockSpec((WINDOW, kvw), prev),
            pl.BlockSpec((WINDOW, kvw), cur_v),
            pl.BlockSpec((WINDOW, kvw), prev_v),
        ],
        out_specs=pl.BlockSpec((WINDOW, nq), lambda b, n: (b * nb + n, 0)),
        compiler_params=_params(("parallel", "parallel")),
        name="swa_prompt",
    )(sinks, bias, qkv, qkv, qkv, qkv, qkv)


def _swa_sample_kernel(dec_t, sink_ref, q_ref, kc_ref, vc_ref, kn_ref, vn_ref, o_ref):
    nrow = SWA_G * dec_t
    scale = SWA_HD ** -0.5
    r = lax.broadcasted_iota(I32, (nrow, 2 * WINDOW), 0)
    j = lax.broadcasted_iota(I32, (nrow, 2 * WINDOW), 1)
    tq = _imod(r, dec_t)
    gq = _idiv(r, dec_t)
    dist = WINDOW + tq - j
    valid = (dist >= 0) & (dist < WINDOW)
    distf = dist.astype(F32)
    g1 = _idiv(lax.broadcasted_iota(I32, (nrow, 1), 0), dec_t)
    npad = 2 * WINDOW - WINDOW - kn_ref.shape[1]
    zpad = jnp.zeros((npad, SWA_KV * SWA_HD), F32)
    for s in range(q_ref.shape[0]):
        kall = jnp.concatenate([kc_ref[s], kn_ref[s], zpad], axis=0).astype(BF16)
        vall = jnp.concatenate([vc_ref[s], vn_ref[s], zpad], axis=0).astype(BF16)
        kbd = _block_diag(kall, SWA_KV)
        vbd = _block_diag(vall, SWA_KV)
        sc_all = _dot_nt(q_ref[s].astype(BF16), kbd)
        ps = []
        for kv in range(SWA_KV):
            slope = jnp.zeros((nrow, 2 * WINDOW), F32)
            sink = jnp.zeros((nrow, 1), F32)
            for g in range(SWA_G):
                slope = jnp.where(gq == g, _slope(kv, g), slope)
                sink = jnp.where(g1 == g, sink_ref[kv * SWA_G + g], sink)
            sc = sc_all[:, kv * 2 * WINDOW:(kv + 1) * 2 * WINDOW] * scale - slope * distf
            sc = jnp.where(valid, sc, NEG_INF)
            m = jnp.maximum(jnp.max(sc, axis=-1, keepdims=True), sink)
            p = jnp.exp(sc - m)
            den = jnp.sum(p, axis=-1, keepdims=True) + jnp.exp(sink - m)
            ps.append((p / den).astype(BF16))
        o_ref[s] = _dot(jnp.concatenate(ps, axis=1), vbd).astype(BF16)


def swa_sample_attention(q_s, k_new, v_new, cache_k, cache_v, sinks, dec_t):
    nbatch, nrow, kvw = q_s.shape
    sb = _tile(nbatch, SWA_SAMPLE_SEQS)
    blk3 = lambda r: pl.BlockSpec((sb, r, kvw), lambda i: (i, 0, 0))
    return pl.pallas_call(
        functools.partial(_swa_sample_kernel, dec_t),
        out_shape=jax.ShapeDtypeStruct((nbatch, nrow, kvw), BF16),
        grid=(nbatch // sb,),
        in_specs=[
            pl.BlockSpec(memory_space=pltpu.SMEM),
            blk3(nrow), blk3(WINDOW), blk3(WINDOW), blk3(k_new.shape[1]), blk3(v_new.shape[1]),
        ],
        out_specs=blk3(nrow),
        compiler_params=_params(("parallel",)),
        name="swa_sample",
    )(sinks, q_s, cache_k, cache_v, k_new, v_new)


def _proj_res_kernel(n_prompt_tiles, split_x, *refs):
    if split_x:
        xp_ref, xs_ref, op_ref, os_ref, w_ref, b_ref, xo_ref, wbf_ref = refs
    else:
        x_ref, op_ref, os_ref, w_ref, b_ref, xo_ref, wbf_ref = refs
    i = pl.program_id(0)

    @pl.when(i == 0)
    def _():
        wbf_ref[...] = w_ref[...].astype(BF16)

    is_prompt = i < n_prompt_tiles
    x = jnp.where(is_prompt, xp_ref[...], xs_ref[...]) if split_x else x_ref[...]
    o = jnp.where(is_prompt, op_ref[...], os_ref[...])
    xo_ref[...] = x + (_dot(o, wbf_ref[...]) + b_ref[...])


def proj_residual(x, o_prompt, o_sample, w, b):
    split_x = isinstance(x, tuple)
    k, d = w.shape
    t = o_prompt.shape[0] + o_sample.shape[0]
    tm = _tile(o_sample.shape[0], 512)
    assert o_prompt.shape[0] % tm == 0
    npt = o_prompt.shape[0] // tm
    x_specs = list(_split_specs(tm, d, npt)) if split_x else [pl.BlockSpec((tm, d), lambda i: (i, 0))]
    x_args = list(x) if split_x else [x]
    return pl.pallas_call(
        functools.partial(_proj_res_kernel, npt, split_x),
        out_shape=jax.ShapeDtypeStruct((t, d), F32),
        grid=(t // tm,),
        in_specs=x_specs + [
            *_split_specs(tm, k, npt),
            pl.BlockSpec((k, d), lambda i: (0, 0)),
            pl.BlockSpec((1, d), lambda i: (0, 0)),
        ],
        out_specs=pl.BlockSpec((tm, d), lambda i: (i, 0)),
        scratch_shapes=[pltpu.VMEM((k, d), BF16)],
        compiler_params=_params(("arbitrary",)),
        name="proj_residual",
    )(*x_args, o_prompt, o_sample, w, b.reshape(1, d))


def _split3_dot(a, w_hi, w_lo):
    a_hi = a.astype(BF16)
    a_lo = (a - a_hi.astype(F32)).astype(BF16)
    return _dot(a_hi, w_hi) + (_dot(a_hi, w_lo) + _dot(a_lo, w_hi))


def _router_kernel(x_ref, g_ref, w_ref, b_ref, f_ref, meta_ref, cnt_ref, whi_ref, wlo_ref, carry_ref):
    i = pl.program_id(0)
    tm = x_ref.shape[0]

    @pl.when(i == 0)
    def _():
        w = w_ref[...]
        hi = w.astype(BF16)
        whi_ref[...] = hi
        wlo_ref[...] = (w - hi.astype(F32)).astype(BF16)
        carry_ref[...] = jnp.zeros_like(carry_ref)

    f = _rms(x_ref[...], g_ref[...])
    f_ref[...] = f
    logits = _split3_dot(f, whi_ref[...], wlo_ref[...]) + b_ref[...]
    lane = lax.broadcasted_iota(I32, (tm, LANES), 1)
    lanef = lane.astype(F32)
    big = float(LANES)

    def first_max(mask):
        v = jnp.max(jnp.where(mask, logits, NEG_INF), axis=-1, keepdims=True)
        idx = jnp.min(jnp.where(mask & (logits == v), lanef, big), axis=-1, keepdims=True)
        return v, idx

    gmask = lane < N_GROUPS
    gmax, gidx = first_max(gmask)
    gsum = jnp.sum(jnp.where(gmask, jnp.exp(logits - gmax), 0.0), axis=-1, keepdims=True)
    g_gate = 1.0 / gsum
    lo = N_GROUPS + EXPERTS_PER_GROUP * gidx
    emask = (lanef >= lo) & (lanef < lo + EXPERTS_PER_GROUP)
    e1, i1 = first_max(emask)
    e2, i2 = first_max(emask & (lanef != i1))
    z = jnp.exp(e2 - e1)
    gate1 = g_gate / (1.0 + z)
    gate2 = g_gate * z / (1.0 + z)
    id1 = i1 - N_GROUPS
    id2 = i2 - N_GROUPS
    oh1 = (lanef == id1).astype(F32)
    oh2 = (lanef == id2).astype(F32)
    oh = oh1 + oh2
    rr = lax.broadcasted_iota(I32, (tm, tm), 0)
    cc = lax.broadcasted_iota(I32, (tm, tm), 1)
    tri = (cc < rr).astype(BF16)
    before = _dot(tri, oh.astype(BF16)) + carry_ref[...]
    rank1 = jnp.sum(oh1 * before, axis=-1, keepdims=True)
    rank2 = jnp.sum(oh2 * before, axis=-1, keepdims=True)
    carry_ref[...] = carry_ref[...] + jnp.sum(oh, axis=0, keepdims=True)
    cnt_ref[...] = carry_ref[...]
    meta = jnp.zeros((tm, LANES), F32)
    for k, v in enumerate((id1, id2, gate1, gate2, rank1, rank2)):
        meta = jnp.where(lane == k, v, meta)
    meta_ref[...] = meta


def moe_router(x, g, w_group, b_group, w_expert, b_expert):
    t, d = x.shape
    tm = _tile(t, 512)
    w = jnp.zeros((d, LANES), F32).at[:, :N_GROUPS].set(w_group).at[:, N_GROUPS:N_GROUPS + N_EXPERTS].set(w_expert)
    b = jnp.zeros((1, LANES), F32).at[0, :N_GROUPS].set(b_group).at[0, N_GROUPS:N_GROUPS + N_EXPERTS].set(b_expert)
    return pl.pallas_call(
        _router_kernel,
        out_shape=(
            jax.ShapeDtypeStruct((t, d), F32),
            jax.ShapeDtypeStruct((t, LANES), F32),
            jax.ShapeDtypeStruct((1, LANES), F32),
        ),
        grid=(t // tm,),
        in_specs=[
            pl.BlockSpec((tm, d), lambda i: (i, 0)),
            pl.BlockSpec((1, d), lambda i: (0, 0)),
            pl.BlockSpec((d, LANES), lambda i: (0, 0)),
            pl.BlockSpec((1, LANES), lambda i: (0, 0)),
        ],
        out_specs=(
            pl.BlockSpec((tm, d), lambda i: (i, 0)),
            pl.BlockSpec((tm, LANES), lambda i: (i, 0)),
            pl.BlockSpec((1, LANES), lambda i: (0, 0)),
        ),
        scratch_shapes=[pltpu.VMEM((d, LANES), BF16), pltpu.VMEM((d, LANES), BF16), pltpu.VMEM((1, LANES), F32)],
        compiler_params=_params(("arbitrary",)),
        name="moe_router",
    )(x, g.reshape(1, d), w, b)


def _dest_kernel(meta_ref, ps_ref, d_ref):
    meta = meta_ref[...]
    tm = meta.shape[0]
    lane = lax.broadcasted_iota(I32, (tm, LANES), 1)
    lanef = lane.astype(F32)
    ps = ps_ref[...]
    both = jnp.zeros((tm, LANES), F32)
    for k in range(TOP_K):
        start = jnp.sum(jnp.where(lanef == meta[:, k:k + 1], ps, 0.0), axis=-1, keepdims=True)
        both = jnp.where(lane == k, start + meta[:, 4 + k:5 + k], both)
    d_ref[0] = both.T[:8, :].astype(I32)


def moe_dest(meta, pstart):
    t = meta.shape[0]
    tm = _tile(t, COMBINE_TILE)
    return pl.pallas_call(
        _dest_kernel,
        out_shape=jax.ShapeDtypeStruct((t // tm, 8, tm), I32),
        grid=(t // tm,),
        in_specs=[pl.BlockSpec((tm, LANES), lambda i: (i, 0)), pl.BlockSpec((1, LANES), lambda i: (0, 0))],
        out_specs=pl.BlockSpec((1, 8, tm), lambda i: (i, 0, 0)),
        compiler_params=_params(("parallel",)),
        name="moe_dest",
    )(meta, pstart)


def _dispatch_kernel(zblk_ref, dcur_ref, dprev_ref, f_hbm, xs_hbm, zero_buf, ring, lsem, sem, zsem):
    i = pl.program_id(0)
    n = pl.num_programs(0)
    tm = dcur_ref.shape[2]
    blk = zero_buf.shape[0]

    nblocks = xs_hbm.shape[0] // blk
    nused = zblk_ref[N_EXPERTS]

    def zero_copy(b):
        return pltpu.make_async_copy(zero_buf, xs_hbm.at[pl.ds(b * blk, blk)], zsem)

    @pl.when(i == 0)
    def _():
        zero_buf[...] = jnp.zeros_like(zero_buf)

        def zero_blocks(wait):
            def last_of_expert(e, carry):
                cp = zero_copy(zblk_ref[e])
                cp.wait() if wait else cp.start()
                return carry

            def unused(b, carry):
                cp = zero_copy(b)
                cp.wait() if wait else cp.start()
                return carry

            lax.fori_loop(0, N_EXPERTS, last_of_expert, 0)
            lax.fori_loop(nused, nblocks, unused, 0)

        zero_blocks(False)
        zero_blocks(True)

    def load(tile):
        slot = tile % 3
        return pltpu.make_async_copy(f_hbm.at[pl.ds(tile * tm, tm)], ring.at[slot], lsem.at[slot])

    def scatter(d_ref, tile, wait):
        slot = tile % 3

        def body(r, carry):
            for k in range(TOP_K):
                cp = pltpu.make_async_copy(ring.at[slot, pl.ds(r, 1)], xs_hbm.at[pl.ds(d_ref[0, k, r], 1)],
                                           sem.at[slot])
                cp.wait() if wait else cp.start()
            return carry
        lax.fori_loop(0, tm, body, 0, unroll=8)

    @pl.when(i == 0)
    def _():
        load(0).start()

    @pl.when(i + 1 < n)
    def _():
        load(i + 1).start()

    load(i).wait()
    scatter(dcur_ref, i, False)

    @pl.when(i > 0)
    def _():
        scatter(dprev_ref, i - 1, True)

    @pl.when(i == n - 1)
    def _():
        scatter(dcur_ref, i, True)


def moe_dispatch(f, dest, zblk, nslot):
    t, d = f.shape
    nt, _, tm = dest.shape
    grid_spec = pltpu.PrefetchScalarGridSpec(
        num_scalar_prefetch=1,
        grid=(nt,),
        in_specs=[
            pl.BlockSpec((1, 8, tm), lambda i, zb: (i, 0, 0), memory_space=pltpu.SMEM),
            pl.BlockSpec((1, 8, tm), lambda i, zb: (jnp.maximum(i - 1, 0), 0, 0), memory_space=pltpu.SMEM),
            pl.BlockSpec(memory_space=pl.ANY),
        ],
        out_specs=pl.BlockSpec(memory_space=pl.ANY),
        scratch_shapes=[pltpu.VMEM((MOE_BLOCK, d), F32), pltpu.VMEM((3, tm, d), F32),
                        pltpu.SemaphoreType.DMA((3,)), pltpu.SemaphoreType.DMA((3,)), pltpu.SemaphoreType.DMA],
    )
    return pl.pallas_call(
        _dispatch_kernel,
        out_shape=jax.ShapeDtypeStruct((nslot, d), F32),
        grid_spec=grid_spec,
        compiler_params=_params(("arbitrary",)),
        name="moe_dispatch",
    )(zblk, dest, dest, f)


def _expert_kernel(blk_e_ref, nblk_ref, x_ref, wg_ref, wu_ref, wd_ref, y_ref, wg_bf, wu_bf, wd_bf):
    j = pl.program_id(0)
    nb = nblk_ref[0]

    @pl.when(j < nb)
    def _():
        e = blk_e_ref[j]
        e_prev = blk_e_ref[jnp.maximum(j - 1, 0)]

        @pl.when(jnp.logical_or(j == 0, e != e_prev))
        def _():
            wg_bf[...] = wg_ref[...].astype(BF16)
            wu_bf[...] = wu_ref[...].astype(BF16)
            wd_bf[...] = wd_ref[...].astype(BF16)

        x = x_ref[...].astype(BF16)
        a = _dot(x, wg_bf[...])
        u = _dot(x, wu_bf[...])
        h = a * (1.0 / (1.0 + jnp.exp(-a))) * u
        y_ref[...] = _dot(h.astype(BF16), wd_bf[...])

    @pl.when(j >= nb)
    def _():
        y_ref[...] = jnp.zeros_like(y_ref)


def moe_experts(xs, blk_e, nblk, layer, w_gate, w_up, w_down):
    nslot, d = xs.shape
    nblocks = nslot // MOE_BLOCK
    de = w_gate.shape[3]
    wmap = lambda j, be, nb: (layer, be[j], 0, 0)
    grid_spec = pltpu.PrefetchScalarGridSpec(
        num_scalar_prefetch=2,
        grid=(nblocks,),
        in_specs=[
            pl.BlockSpec((MOE_BLOCK, d), lambda j, be, nb: (jnp.minimum(j, nb[0] - 1), 0)),
            pl.BlockSpec((None, None, d, de), wmap),
            pl.BlockSpec((None, None, d, de), wmap),
            pl.BlockSpec((None, None, de, d), wmap),
        ],
        out_specs=pl.BlockSpec((MOE_BLOCK, d), lambda j, be, nb: (j, 0)),
        scratch_shapes=[pltpu.VMEM((d, de), BF16), pltpu.VMEM((d, de), BF16), pltpu.VMEM((de, d), BF16)],
    )
    return pl.pallas_call(
        _expert_kernel,
        out_shape=jax.ShapeDtypeStruct((nslot, d), F32),
        grid_spec=grid_spec,
        compiler_params=_params(("arbitrary",)),
        name="moe_experts",
    )(blk_e, nblk, xs, w_gate, w_up, w_down)


def _combine_kernel(n_prompt_tiles, dcur_ref, dnxt_ref, yb_hbm, x_ref, meta_ref, g_ref, *refs):
    final_norm = n_prompt_tiles is not None
    if final_norm:
        op_ref, os_ref, ybuf, sem = refs
    else:
        o_ref, ybuf, sem = refs
    i = pl.program_id(0)
    n = pl.num_programs(0)
    tm = x_ref.shape[0]

    def gather(d_ref, slot, wait):
        def body(r, carry):
            for k in range(TOP_K):
                cp = pltpu.make_async_copy(yb_hbm.at[pl.ds(d_ref[0, k, r], 1)], ybuf.at[slot, k, pl.ds(r, 1)],
                                           sem.at[slot])
                cp.wait() if wait else cp.start()
            return carry
        lax.fori_loop(0, tm, body, 0, unroll=8)

    @pl.when(i == 0)
    def _():
        gather(dcur_ref, 0, False)

    @pl.when(i + 1 < n)
    def _():
        gather(dnxt_ref, (i + 1) % 2, False)

    slot = i % 2
    gather(dcur_ref, slot, True)
    meta = meta_ref[...]
    gate1 = meta[:, 2:3]
    gate2 = meta[:, 3:4]
    out = x_ref[...] + (ybuf[slot, 0] * gate1 + ybuf[slot, 1] * gate2)
    if final_norm:
        out = _rms(out, g_ref[...])

        @pl.when(i < n_prompt_tiles)
        def _():
            op_ref[...] = out

        @pl.when(i >= n_prompt_tiles)
        def _():
            os_ref[...] = out
    else:
        o_ref[...] = out


def moe_combine(x, yb, dest, meta, g_final, n_prompt):
    t, d = x.shape
    nt, _, tm = dest.shape
    cur = pl.BlockSpec((1, 8, tm), lambda i: (i, 0, 0), memory_space=pltpu.SMEM)
    nxt = pl.BlockSpec((1, 8, tm), lambda i: (jnp.minimum(i + 1, nt - 1), 0, 0), memory_space=pltpu.SMEM)
    if n_prompt is None:
        npt = None
        out_shape = jax.ShapeDtypeStruct((t, d), F32)
        out_specs = pl.BlockSpec((tm, d), lambda i: (i, 0))
    else:
        assert n_prompt % tm == 0
        npt = n_prompt // tm
        out_shape = (jax.ShapeDtypeStruct((n_prompt, d), F32), jax.ShapeDtypeStruct((t - n_prompt, d), F32))
        out_specs = _split_specs(tm, d, npt)
    return pl.pallas_call(
        functools.partial(_combine_kernel, npt),
        out_shape=out_shape,
        grid=(nt,),
        in_specs=[
            cur, nxt,
            pl.BlockSpec(memory_space=pl.ANY),
            pl.BlockSpec((tm, d), lambda i: (i, 0)),
            pl.BlockSpec((tm, LANES), lambda i: (i, 0)),
            pl.BlockSpec((1, d), lambda i: (0, 0)),
        ],
        out_specs=out_specs,
        scratch_shapes=[pltpu.VMEM((2, TOP_K, tm, d), F32), pltpu.SemaphoreType.DMA((2,))],
        compiler_params=_params(("arbitrary",)),
        name="moe_combine",
    )(dest, dest, yb, x, meta, g_final.reshape(1, d))


def hier_moe_layer(x, layer, norm_g, w_group, b_group, w_expert, b_expert, w_gate, w_up, w_down, g_final,
                   n_prompt):
    t, d = x.shape
    f, meta, counts = moe_router(x, norm_g, w_group, b_group, w_expert, b_expert)
    cnt = counts[0, :N_EXPERTS].astype(I32)
    padded = (cnt + MOE_BLOCK - 1) // MOE_BLOCK * MOE_BLOCK
    pend = jnp.cumsum(padded)
    pstart = pend - padded
    nblocks = -(-(t * TOP_K) // MOE_BLOCK) + N_EXPERTS
    blk_start = jnp.arange(nblocks, dtype=I32) * MOE_BLOCK
    blk_e = jnp.minimum(jnp.sum((pend[None, :] <= blk_start[:, None]).astype(I32), axis=1), N_EXPERTS - 1)
    nblk = (pend[-1:] // MOE_BLOCK).astype(I32)
    zblk = jnp.concatenate([jnp.maximum(pend // MOE_BLOCK - 1, 0), pend[-1:] // MOE_BLOCK]).astype(I32)
    ps_row = jnp.concatenate([pstart.astype(F32), jnp.zeros((LANES - N_EXPERTS,), F32)]).reshape(1, LANES)
    dest = moe_dest(meta, ps_row)
    xs = moe_dispatch(f, dest, zblk, nblocks * MOE_BLOCK)
    yb = moe_experts(xs, blk_e, nblk, layer, w_gate, w_up, w_down)
    return moe_combine(x, yb, dest, meta, g_final, n_prompt)


def _mla_down_kernel(x_ref, g_ref, w_ref, gq_ref, gkv_ref, cos_ref, sin_ref, cq_ref, ckr_ref, wbf_ref):
    @pl.when(pl.program_id(0) == 0)
    def _():
        wbf_ref[...] = w_ref[...].astype(BF16)

    h = _rms(x_ref[...], g_ref[...])
    a = _dot(h.astype(BF16), wbf_ref[...])
    q0, c0 = MLA_Q_LORA, MLA_Q_LORA + MLA_KV_LORA
    cq_ref[...] = _rms(a[:, :q0], gq_ref[...]).astype(BF16)
    ckr_ref[:, :MLA_KV_LORA] = _rms(a[:, q0:c0], gkv_ref[...])
    ckr_ref[:, MLA_KV_LORA:] = a[:, c0:c0 + LANES] * cos_ref[...] + a[:, c0 + LANES:] * sin_ref[...]


def _rot_cols(w):
    half = MLA_ROPE // 2
    return jnp.concatenate([-w[..., half:], w[..., :half]], axis=-1)


def mla_down(x, g, w_dqkv, norm_q, norm_kv, cos_t, sin_t):
    t, d = x.shape
    tm = _tile(t, 512)
    q0, c0 = MLA_Q_LORA, MLA_Q_LORA + MLA_KV_LORA
    w_r = w_dqkv[:, c0:]
    zpad = jnp.zeros((d, LANES - MLA_ROPE), F32)
    w = jnp.concatenate([w_dqkv[:, :c0], w_r, zpad, _rot_cols(w_r), zpad], axis=1)
    n = w.shape[1]
    return pl.pallas_call(
        _mla_down_kernel,
        out_shape=(jax.ShapeDtypeStruct((t, q0), BF16), jax.ShapeDtypeStruct((t, MLA_KV_LORA + LANES), F32)),
        grid=(t // tm,),
        in_specs=[
            pl.BlockSpec((tm, d), lambda i: (i, 0)),
            pl.BlockSpec((1, d), lambda i: (0, 0)),
            pl.BlockSpec((d, n), lambda i: (0, 0)),
            pl.BlockSpec((1, q0), lambda i: (0, 0)),
            pl.BlockSpec((1, MLA_KV_LORA), lambda i: (0, 0)),
            pl.BlockSpec((tm, LANES), lambda i: (i, 0)),
            pl.BlockSpec((tm, LANES), lambda i: (i, 0)),
        ],
        out_specs=(pl.BlockSpec((tm, q0), lambda i: (i, 0)),
                   pl.BlockSpec((tm, MLA_KV_LORA + LANES), lambda i: (i, 0))),
        scratch_shapes=[pltpu.VMEM((d, n), BF16)],
        compiler_params=_params(("arbitrary",)),
        name="mla_down",
    )(x, g.reshape(1, d), w, norm_q.reshape(1, q0), norm_kv.reshape(1, MLA_KV_LORA), cos_t, sin_t)


def _mla_up_kernel(cq_ref, ckr_ref, qc_ref, qs_ref, wq_ref, wk_ref, wv_ref, q_ref, k_ref, v_ref,
                   wq_bf, wk_bf, wv_bf):
    @pl.when(pl.program_id(0) == 0)
    def _():
        wq_bf[...] = wq_ref[...].astype(BF16)
        wk_bf[...] = wk_ref[...].astype(BF16)
        wv_bf[...] = wv_ref[...].astype(BF16)

    nq = q_ref.shape[1]
    ab = _dot(cq_ref[...], wq_bf[...])
    qc = qc_ref[...]
    qs = qs_ref[...]
    for h in range(MLA_HEADS):
        sl = slice(h * LANES, (h + 1) * LANES)
        sl_b = slice(nq + h * LANES, nq + (h + 1) * LANES)
        q_ref[:, sl] = (ab[:, sl] * qc + ab[:, sl_b] * qs).astype(BF16)
    ckr = ckr_ref[...].astype(BF16)
    k_ref[...] = _dot(ckr, wk_bf[...]).astype(BF16)
    v_ref[...] = _dot(ckr[:, :MLA_KV_LORA], wv_bf[...]).astype(BF16)


def _head_slots(w, lo):
    r, h, n = w.shape
    out = jnp.zeros((r, h, LANES), F32).at[:, :, lo:lo + n].set(w)
    return out.reshape(r, h * LANES)


def mla_up(cq, ckr, q_cos, q_sin, w_uq, w_uk, w_uv):
    t = cq.shape[0]
    tm = _tile(t, 256)
    nq = MLA_HEADS * LANES
    kin = MLA_KV_LORA + LANES
    wq3 = w_uq.reshape(MLA_Q_LORA, MLA_HEADS, MLA_NOPE + MLA_ROPE)
    wq_a = _head_slots(wq3, 0)
    wq_b = _head_slots(_rot_cols(wq3[:, :, MLA_NOPE:]), MLA_NOPE)
    wq = jnp.concatenate([wq_a, wq_b], axis=1)
    place = jnp.zeros((LANES, MLA_HEADS, LANES), F32)
    place = place.at[jnp.arange(MLA_ROPE), :, MLA_NOPE + jnp.arange(MLA_ROPE)].set(1.0)
    wk = jnp.concatenate([_head_slots(w_uk, 0), place.reshape(LANES, nq)], axis=0)
    even = (jnp.arange(MLA_HEADS) % 2 == 0)[None, :, None]
    wv = jnp.where(even, _head_slots(w_uv, 0).reshape(MLA_KV_LORA, MLA_HEADS, LANES),
                   _head_slots(w_uv, MLA_V).reshape(MLA_KV_LORA, MLA_HEADS, LANES)).reshape(MLA_KV_LORA, nq)
    full = lambda r, c: pl.BlockSpec((r, c), lambda i: (0, 0))
    rows = lambda c: pl.BlockSpec((tm, c), lambda i: (i, 0))
    return pl.pallas_call(
        _mla_up_kernel,
        out_shape=tuple(jax.ShapeDtypeStruct((t, nq), BF16) for _ in range(3)),
        grid=(t // tm,),
        in_specs=[rows(MLA_Q_LORA), rows(kin), rows(LANES), rows(LANES),
                  full(MLA_Q_LORA, 2 * nq), full(kin, nq), full(MLA_KV_LORA, nq)],
        out_specs=(rows(nq), rows(nq), rows(nq)),
        scratch_shapes=[pltpu.VMEM((MLA_Q_LORA, 2 * nq), BF16), pltpu.VMEM((kin, nq), BF16),
                        pltpu.VMEM((MLA_KV_LORA, nq), BF16)],
        compiler_params=_params(("arbitrary",)),
        name="mla_up",
    )(cq, ckr, q_cos, q_sin, wq, wk, wv)


def _mla_flash_kernel(q_ref, k_ref, v_ref, o_ref, m_ref, l_ref, acc_ref):
    qi = pl.program_id(2)
    tq = q_ref.shape[0]
    lane = lax.broadcasted_iota(I32, (tq, LANES), 1)
    first_half = lane < MLA_V
    m_ref[...] = jnp.full_like(m_ref, NEG_INF)
    l_ref[...] = jnp.zeros_like(l_ref)
    acc_ref[...] = jnp.zeros_like(acc_ref)
    q = q_ref[...]

    def step(kb, masked):
        start = pl.multiple_of(kb * tq, tq)
        kblk = k_ref[pl.ds(start, tq), :]
        vblk = v_ref[pl.ds(start, tq), :]
        pv = None
        alphas = []
        for h in range(2):
            s = _dot_nt(q[:, h * LANES:(h + 1) * LANES], kblk[:, h * LANES:(h + 1) * LANES])
            if masked:
                row = lax.broadcasted_iota(I32, s.shape, 0)
                col = lax.broadcasted_iota(I32, s.shape, 1)
                s = jnp.where(col <= row, s, NEG_INF)
            sc = [s[:, c * LANES:(c + 1) * LANES] for c in range(tq // LANES)]
            m_old = m_ref[h]
            m_new = jnp.maximum(m_old, jnp.max(functools.reduce(jnp.maximum, sc), axis=-1, keepdims=True))
            alpha = jnp.exp2(m_old - m_new)
            pc = [jnp.exp2(x - m_new) for x in sc]
            l_ref[h] = alpha * l_ref[h] + functools.reduce(jnp.add, pc)
            m_ref[h] = m_new
            p = jnp.concatenate([x.astype(BF16) for x in pc], axis=1)
            part = _dot(p, vblk[:, h * LANES:(h + 1) * LANES])
            pv = part if pv is None else pv + part
            alphas.append(alpha)
        acc_ref[...] = jnp.where(first_half, alphas[0], alphas[1]) * acc_ref[...] + pv

    def body(kb, carry):
        step(kb, False)
        return carry

    lax.fori_loop(0, qi, body, 0)
    step(qi, True)
    l0 = jnp.sum(l_ref[0], axis=-1, keepdims=True)
    l1 = jnp.sum(l_ref[1], axis=-1, keepdims=True)
    o_ref[...] = (acc_ref[...] * jnp.where(first_half, 1.0 / l0, 1.0 / l1)).astype(BF16)


def mla_prompt_attention(qp, kp, vp, batch, seq):
    tq = _tile(seq, MLA_TQ)
    nqb = seq // tq
    npair = MLA_HEADS // 2
    pw = 2 * LANES
    return pl.pallas_call(
        _mla_flash_kernel,
        out_shape=jax.ShapeDtypeStruct((batch * seq, MLA_HEADS * MLA_V), BF16),
        grid=(batch, npair, nqb),
        in_specs=[
            pl.BlockSpec((tq, pw), lambda b, h, i: (b * nqb + i, h)),
            pl.BlockSpec((seq, pw), lambda b, h, i: (b, h)),
            pl.BlockSpec((seq, pw), lambda b, h, i: (b, h)),
        ],
        out_specs=pl.BlockSpec((tq, LANES), lambda b, h, i: (b * nqb + i, h)),
        scratch_shapes=[pltpu.VMEM((2, tq, LANES), F32), pltpu.VMEM((2, tq, LANES), F32),
                        pltpu.VMEM((tq, LANES), F32)],
        compiler_params=_params(("parallel", "parallel", "arbitrary")),
        name="mla_prompt",
    )(qp, kp, vp)


def _mla_absorb_kernel(q_ref, w_ref, o_ref):
    o_ref[0] = _dot(q_ref[...], w_ref[0].astype(BF16)).astype(BF16)


def mla_absorb(q_s, w_uk):
    ns = q_s.shape[0]
    kin = MLA_KV_LORA + LANES
    w = jnp.zeros((MLA_HEADS, LANES, kin), F32)
    w = w.at[:, :MLA_NOPE, :MLA_KV_LORA].set(jnp.transpose(w_uk, (1, 2, 0)))
    w = w.at[:, MLA_NOPE + jnp.arange(MLA_ROPE), MLA_KV_LORA + jnp.arange(MLA_ROPE)].set(1.0)
    return pl.pallas_call(
        _mla_absorb_kernel,
        out_shape=jax.ShapeDtypeStruct((MLA_HEADS, ns, kin), BF16),
        grid=(MLA_HEADS,),
        in_specs=[pl.BlockSpec((ns, LANES), lambda h: (0, h)),
                  pl.BlockSpec((1, LANES, kin), lambda h: (h, 0, 0))],
        out_specs=pl.BlockSpec((1, ns, kin), lambda h: (h, 0, 0)),
        compiler_params=_params(("parallel",)),
        name="mla_absorb",
    )(q_s, w)


def _mla_decode_kernel(npg, dec_t, pt_ref, q_ref, new_ref, lat_hbm, krt_hbm, o_ref,
                       lat_buf, krt_buf, sem, s_buf, latb_buf):
    b = pl.program_id(0)
    nb = pl.num_programs(0)
    npages = pt_ref.shape[1]
    page = lat_hbm.shape[1]
    ck = npg * page
    nchunk = npages // npg
    nrow = q_ref.shape[0]

    def page_copies(bb, slot, i):
        pg = pt_ref[bb, i]
        return (pltpu.make_async_copy(lat_hbm.at[pg], lat_buf.at[slot, pl.ds(i * page, page)], sem.at[slot]),
                pltpu.make_async_copy(krt_hbm.at[pg], krt_buf.at[slot, pl.ds(i * MLA_ROPE, MLA_ROPE)],
                                      sem.at[slot]))

    def fetch(bb, slot, wait):
        def body(i, carry):
            for cp in page_copies(bb, slot, i):
                cp.wait() if wait else cp.start()
            return carry
        lax.fori_loop(0, npages, body, 0, unroll=8)

    @pl.when(b == 0)
    def _():
        fetch(0, 0, False)

    @pl.when(b + 1 < nb)
    def _():
        fetch(b + 1, (b + 1) % 2, False)

    slot = b % 2
    fetch(b, slot, True)

    q = q_ref[...]
    ql = q[:, :MLA_KV_LORA]
    qr = q[:, MLA_KV_LORA:MLA_KV_LORA + MLA_ROPE]
    mloc = jnp.full((nrow, LANES), NEG_INF, F32)
    for c in range(nchunk):
        lat = lat_buf[slot, c * ck:(c + 1) * ck, :].astype(BF16)
        latb_buf[c * ck:(c + 1) * ck, :] = lat
        krt = jnp.concatenate(
            [krt_buf[slot, (c * npg + i) * MLA_ROPE:(c * npg + i + 1) * MLA_ROPE, :] for i in range(npg)],
            axis=1).astype(BF16)
        s = _dot_nt(ql, lat) + _dot(qr, krt)
        s_buf[:, c * ck:(c + 1) * ck] = s
        for j in range(ck // LANES):
            mloc = jnp.maximum(mloc, s[:, j * LANES:(j + 1) * LANES])

    new = new_ref[...].astype(BF16)
    sn = _dot_nt(q, new)
    t_row = _idiv(lax.broadcasted_iota(I32, sn.shape, 0), MLA_HEADS)
    t_col = lax.broadcasted_iota(I32, sn.shape, 1)
    sn = jnp.where(t_col <= t_row, sn, NEG_INF)
    m = jnp.maximum(jnp.max(mloc, axis=-1, keepdims=True), jnp.max(sn, axis=-1, keepdims=True))
    pn = jnp.exp2(sn - m)
    lsum = jnp.zeros((nrow, LANES), F32)
    acc = jnp.zeros((nrow, MLA_KV_LORA), F32)
    for c in range(nchunk):
        pc = [jnp.exp2(s_buf[:, c * ck + j * LANES:c * ck + (j + 1) * LANES] - m) for j in range(ck // LANES)]
        lsum = lsum + functools.reduce(jnp.add, pc)
        p = jnp.concatenate([x.astype(BF16) for x in pc], axis=1)
        acc = acc + _dot(p, latb_buf[c * ck:(c + 1) * ck, :])
    l = jnp.sum(lsum, axis=-1, keepdims=True) + jnp.sum(pn, axis=-1, keepdims=True)
    pnb = pn.astype(BF16).astype(F32)
    cn = new[:, :MLA_KV_LORA].astype(F32)
    for t in range(dec_t):
        acc = acc + pnb[:, t:t + 1] * cn[t:t + 1, :]
    o_ref[...] = (acc / l).astype(BF16)


def mla_decode(qabs, ckr_new, cache_lat, cache_krt, page_table, dec_t):
    nbatch, npages = page_table.shape
    npg = _tile(npages, MLA_PAGES_PER_STEP)
    nrow = dec_t * MLA_HEADS
    kin = MLA_KV_LORA + LANES
    page = cache_lat.shape[1]
    nkeys = npages * page
    grid_spec = pltpu.PrefetchScalarGridSpec(
        num_scalar_prefetch=1,
        grid=(nbatch,),
        in_specs=[pl.BlockSpec((nrow, kin), lambda b, pt: (b, 0)),
                  pl.BlockSpec((None, 8, kin), lambda b, pt: (b, 0, 0)),
                  pl.BlockSpec(memory_space=pl.ANY),
                  pl.BlockSpec(memory_space=pl.ANY)],
        out_specs=pl.BlockSpec((nrow, MLA_KV_LORA), lambda b, pt: (b, 0)),
        scratch_shapes=[pltpu.VMEM((2, nkeys, MLA_KV_LORA), F32),
                        pltpu.VMEM((2, npages * MLA_ROPE, page), F32),
                        pltpu.SemaphoreType.DMA((2,)),
                        pltpu.VMEM((nrow, nkeys), F32),
                        pltpu.VMEM((nkeys, MLA_KV_LORA), BF16)],
    )
    return pl.pallas_call(
        functools.partial(_mla_decode_kernel, npg, dec_t),
        out_shape=jax.ShapeDtypeStruct((nbatch * nrow, MLA_KV_LORA), BF16),
        grid_spec=grid_spec,
        compiler_params=_params(("arbitrary",)),
        name="mla_decode",
    )(page_table, qabs, ckr_new, cache_lat, cache_krt)


def _mla_unabsorb_kernel(ol_ref, w_ref, o_ref):
    r = _dot(ol_ref[...], w_ref[...].astype(BF16))
    row_h = _imod(lax.broadcasted_iota(I32, r.shape, 0), MLA_HEADS)
    col_h = _idiv(lax.broadcasted_iota(I32, r.shape, 1), MLA_V)
    r = jnp.where(row_h == col_h, r, 0.0)
    o_ref[...] = jnp.sum(r.reshape(r.shape[0] // MLA_HEADS, MLA_HEADS, r.shape[1]), axis=1).astype(BF16)


def mla_unabsorb(o_lat, w_uv):
    rows = o_lat.shape[0]
    tr = _tile(rows, 1024)
    n = MLA_HEADS * MLA_V
    return pl.pallas_call(
        _mla_unabsorb_kernel,
        out_shape=jax.ShapeDtypeStruct((rows // MLA_HEADS, n), BF16),
        grid=(rows // tr,),
        in_specs=[pl.BlockSpec((tr, MLA_KV_LORA), lambda i: (i, 0)),
                  pl.BlockSpec((MLA_KV_LORA, n), lambda i: (0, 0))],
        out_specs=pl.BlockSpec((tr // MLA_HEADS, n), lambda i: (i, 0)),
        compiler_params=_params(("parallel",)),
        name="mla_unabsorb",
    )(o_lat, w_uv.reshape(MLA_KV_LORA, n))


def _rope_tables(pos):
    half = MLA_ROPE // 2
    inv = jnp.power(ROPE_THETA, -jnp.arange(half, dtype=F32) * 2.0 / MLA_ROPE)
    ang = pos[:, None] * inv[None, :]
    cos2 = jnp.tile(jnp.cos(ang), (1, 2))
    sin2 = jnp.tile(jnp.sin(ang), (1, 2))
    t = pos.shape[0]
    scale = (MLA_NOPE + MLA_ROPE) ** -0.5 * LOG2E
    zeros = lambda n: jnp.zeros((t, n), F32)
    k_cos = jnp.concatenate([cos2, zeros(LANES - MLA_ROPE)], axis=1)
    k_sin = jnp.concatenate([sin2, zeros(LANES - MLA_ROPE)], axis=1)
    tail = LANES - MLA_NOPE - MLA_ROPE
    q_cos = jnp.concatenate([jnp.full((t, MLA_NOPE), scale, F32), scale * cos2, zeros(tail)], axis=1)
    q_sin = jnp.concatenate([zeros(MLA_NOPE), scale * sin2, zeros(tail)], axis=1)
    return k_cos, k_sin, q_cos, q_sin


def kernel(x_prompt, x_sample, cache_swa_k, cache_swa_v, cache_mla_latent, cache_mla_krope, page_table,
           norm_attn, norm_ffn, norm_final,
           swa_w_qkv, swa_b_qkv, swa_sinks, swa_w_o, swa_b_o,
           mla_w_dqkv, mla_norm_q, mla_norm_kv, mla_w_uq, mla_w_uk, mla_w_uv, mla_w_o,
           moe_w_group, moe_b_group, moe_w_expert, moe_b_expert, moe_w_gate, moe_w_up, moe_w_down):
    batch, seq, d = x_prompt.shape
    nbatch, dec_t, _ = x_sample.shape
    n_p = batch * seq
    n_s = nbatch * dec_t
    npages = page_table.shape[1]
    page = cache_mla_latent.shape[2]
    past_len = npages * page
    nq = SWA_HEADS * SWA_HD
    kvw = SWA_KV * SWA_HD
    x_in = (x_prompt.reshape(n_p, d), x_sample.reshape(n_s, d))

    w_qkv = swa_w_qkv[0]
    w_q = w_qkv[:, :nq].reshape(d, SWA_KV, SWA_G, SWA_HD).transpose(0, 2, 1, 3).reshape(d, nq)
    b_q = swa_b_qkv[0][:nq].reshape(SWA_KV, SWA_G, SWA_HD).transpose(1, 0, 2).reshape(nq)
    w0 = jnp.concatenate([w_q, w_qkv[:, nq:]], axis=1)
    b0 = jnp.concatenate([b_q, swa_b_qkv[0][nq:]])
    w_o0 = swa_w_o[0].reshape(SWA_KV, SWA_G, SWA_HD, d).transpose(1, 0, 2, 3).reshape(nq, d)
    qkv = norm_proj(*x_in, norm_attn[0], w0, b0)
    o_p = swa_prompt_attention(qkv, swa_sinks[0], batch, seq)
    qkv_s = qkv[n_p:]
    q_s = qkv_s[:, :nq].reshape(nbatch, dec_t, SWA_G, kvw).transpose(0, 2, 1, 3).reshape(nbatch, SWA_G * dec_t, kvw)
    k_s = qkv_s[:, nq:nq + kvw].reshape(nbatch, dec_t, kvw)
    v_s = qkv_s[:, nq + kvw:].reshape(nbatch, dec_t, kvw)
    pad8 = lambda a: jnp.pad(a, ((0, 0), (0, 8 - dec_t), (0, 0)))
    ck = cache_swa_k[0].reshape(nbatch, WINDOW, kvw)
    cv = cache_swa_v[0].reshape(nbatch, WINDOW, kvw)
    o_s = swa_sample_attention(q_s, pad8(k_s), pad8(v_s), ck, cv, swa_sinks[0], dec_t)
    o_s = o_s.reshape(nbatch, SWA_G, dec_t, kvw).transpose(0, 2, 1, 3).reshape(n_s, nq)
    x = proj_residual(x_in, o_p, o_s, w_o0, swa_b_o[0])
    x = hier_moe_layer(x, 0, norm_ffn[0], moe_w_group[0], moe_b_group[0], moe_w_expert[0], moe_b_expert[0],
                       moe_w_gate, moe_w_up, moe_w_down, norm_final, None)

    last = jnp.stack([qkv[(b + 1) * seq - WINDOW:(b + 1) * seq, nq:] for b in range(batch)])
    swa_kp = last[:, :, :kvw].reshape(1, batch, WINDOW, SWA_KV, SWA_HD)
    swa_vp = last[:, :, kvw:].reshape(1, batch, WINDOW, SWA_KV, SWA_HD)
    swa_ks = jnp.concatenate([cache_swa_k[0], k_s.reshape(nbatch, dec_t, SWA_KV, SWA_HD)], axis=1)[:, dec_t:][None]
    swa_vs = jnp.concatenate([cache_swa_v[0], v_s.reshape(nbatch, dec_t, SWA_KV, SWA_HD)], axis=1)[:, dec_t:][None]

    pos = jnp.concatenate([jnp.tile(jnp.arange(seq, dtype=F32), batch),
                           jnp.tile(past_len + jnp.arange(dec_t, dtype=F32), nbatch)])
    k_cos, k_sin, q_cos, q_sin = _rope_tables(pos)
    cq, ckr = mla_down(x, norm_attn[1], mla_w_dqkv[0], mla_norm_q[0], mla_norm_kv[0], k_cos, k_sin)
    qp, kp, vp = mla_up(cq, ckr, q_cos, q_sin, mla_w_uq[0], mla_w_uk[0], mla_w_uv[0])
    o_p = mla_prompt_attention(qp, kp, vp, batch, seq)
    qabs = mla_absorb(qp[n_p:], mla_w_uk[0])
    qabs = qabs.transpose(1, 0, 2).reshape(n_s * MLA_HEADS, MLA_KV_LORA + LANES)
    ckr_new = jnp.pad(ckr[n_p:].reshape(nbatch, dec_t, MLA_KV_LORA + LANES), ((0, 0), (0, 8 - dec_t), (0, 0)))
    o_lat = mla_decode(qabs, ckr_new, cache_mla_latent[0], jnp.swapaxes(cache_mla_krope[0], 1, 2), page_table, dec_t)
    o_s = mla_unabsorb(o_lat, mla_w_uv[0])
    x = proj_residual(x, o_p, o_s, mla_w_o[0], jnp.zeros((d,), F32))
    y_p, y_s = hier_moe_layer(x, 1, norm_ffn[1], moe_w_group[1], moe_b_group[1], moe_w_expert[1], moe_b_expert[1],
                              moe_w_gate, moe_w_up, moe_w_down, norm_final, n_p)

    c_all = ckr[:, :MLA_KV_LORA]
    r_all = ckr[:, MLA_KV_LORA:MLA_KV_LORA + MLA_ROPE]
    return (y_p.reshape(batch, seq, d), y_s.reshape(nbatch, dec_t, d),
            swa_kp, swa_vp, swa_ks, swa_vs,
            c_all[:n_p].reshape(1, batch, seq, MLA_KV_LORA), r_all[:n_p].reshape(1, batch, seq, MLA_ROPE),
            c_all[n_p:].reshape(1, nbatch, dec_t, MLA_KV_LORA), r_all[n_p:].reshape(1, nbatch, dec_t, MLA_ROPE))
```

```python
import functools

import jax
import jax.numpy as jnp
import numpy as np
from jax import lax
from jax.experimental import pallas as pl
from jax.experimental.pallas import tpu as pltpu

F32 = jnp.float32
BF16 = jnp.bfloat16
I32 = jnp.int32

SWA_HEADS = 16
SWA_KV = 4
SWA_G = SWA_HEADS // SWA_KV
SWA_HD = 64
WINDOW = 128
MLA_HEADS = 16
MLA_Q_LORA = 384
MLA_KV_LORA = 256
MLA_NOPE = 64
MLA_ROPE = 32
MLA_V = 64
ROPE_THETA = 10000.0
N_GROUPS = 8
EXPERTS_PER_GROUP = 8
N_EXPERTS = N_GROUPS * EXPERTS_PER_GROUP
TOP_K = 2
RMS_EPS = 1e-6
NEG_INF = -1e30
LOG2E = 1.4426950408889634

LANES = 128
VMEM_LIMIT_BYTES = 52 * 1024 * 1024

MOE_BLOCK = 256
COMBINE_TILE = 512
MLA_TQ = 1024
MLA_PAGES_PER_STEP = 8
SWA_SAMPLE_SEQS = 8


def _tile(n, pref):
    t = pref
    while t > 8 and n % t:
        t //= 2
    assert n % t == 0, (n, pref)
    return t


def _params(sem, vmem=VMEM_LIMIT_BYTES):
    return pltpu.CompilerParams(dimension_semantics=sem, vmem_limit_bytes=vmem)


def _log2(n):
    assert n > 0 and n & (n - 1) == 0, n
    return n.bit_length() - 1


def _idiv(v, n):
    return lax.shift_right_logical(v, _log2(n))


def _imod(v, n):
    assert n & (n - 1) == 0, n
    return jnp.bitwise_and(v, n - 1)


def _rms(x, g):
    return x * lax.rsqrt(jnp.mean(x * x, axis=-1, keepdims=True) + RMS_EPS) * g


def _dot(a, b):
    return jnp.dot(a, b, preferred_element_type=F32)


def _dot_nt(a, b):
    return lax.dot_general(a, b, (((1,), (1,)), ((), ())), preferred_element_type=F32)


def _split_specs(tm, width, n_first_tiles):
    return (pl.BlockSpec((tm, width), lambda i: (jnp.minimum(i, n_first_tiles - 1), 0)),
            pl.BlockSpec((tm, width), lambda i: (jnp.maximum(i - n_first_tiles, 0), 0)))


def _norm_proj_kernel(n_prompt_tiles, xp_ref, xs_ref, g_ref, w_ref, b_ref, o_ref, wbf_ref):
    i = pl.program_id(0)

    @pl.when(i == 0)
    def _():
        wbf_ref[...] = w_ref[...].astype(BF16)

    x = jnp.where(i < n_prompt_tiles, xp_ref[...], xs_ref[...])
    h = _rms(x, g_ref[...])
    o_ref[...] = _dot(h.astype(BF16), wbf_ref[...]) + b_ref[...]


def norm_proj(x_p, x_s, g, w, b):
    d = x_p.shape[1]
    t = x_p.shape[0] + x_s.shape[0]
    n = w.shape[1]
    tm = _tile(x_s.shape[0], 512)
    assert x_p.shape[0] % tm == 0
    npt = x_p.shape[0] // tm
    return pl.pallas_call(
        functools.partial(_norm_proj_kernel, npt),
        out_shape=jax.ShapeDtypeStruct((t, n), F32),
        grid=(t // tm,),
        in_specs=[
            *_split_specs(tm, d, npt),
            pl.BlockSpec((1, d), lambda i: (0, 0)),
            pl.BlockSpec((d, n), lambda i: (0, 0)),
            pl.BlockSpec((1, n), lambda i: (0, 0)),
        ],
        out_specs=pl.BlockSpec((tm, n), lambda i: (i, 0)),
        scratch_shapes=[pltpu.VMEM((d, n), BF16)],
        compiler_params=_params(("arbitrary",)),
        name="norm_proj",
    )(x_p, x_s, g.reshape(1, d), w, b.reshape(1, n))


def _block_diag(x, nseg):
    w = x.shape[1] // nseg
    seg = _idiv(lax.broadcasted_iota(I32, x.shape, 1), w)
    zero = jnp.zeros_like(x)
    return jnp.concatenate([jnp.where(seg == s, x, zero) for s in range(nseg)], axis=0)


def _slope(kv, g):
    return 2.0 ** (-8.0 * (kv * SWA_G + g + 1) / SWA_HEADS)


def _swa_bias_tables():
    qi = np.arange(WINDOW)[:, None]
    kj = np.arange(2 * WINDOW)[None, :]
    dist = qi + WINDOW - kj
    band = (dist >= 0) & (dist < WINDOW)
    out = np.empty((2, SWA_G * SWA_KV, WINDOW, 2 * WINDOW), np.float32)
    for g in range(SWA_G):
        for kv in range(SWA_KV):
            bias = -_slope(kv, g) * dist * LOG2E
            out[1, g * SWA_KV + kv] = np.where(band, bias, NEG_INF)
            out[0, g * SWA_KV + kv] = np.where(band & (kj >= WINDOW), bias, NEG_INF)
    return out


def _swa_prompt_kernel(sink_ref, bias_ref, q_ref, kc_ref, kp_ref, vc_ref, vp_ref, o_ref):
    kvw = SWA_KV * SWA_HD
    k2 = jnp.concatenate([kp_ref[...], kc_ref[...]], axis=0).astype(BF16)
    v2 = jnp.concatenate([vp_ref[...], vc_ref[...]], axis=0).astype(BF16)
    kbd = _block_diag(k2, SWA_KV)
    vbd = _block_diag(v2, SWA_KV)
    qall = jnp.concatenate([q_ref[:, g * kvw:(g + 1) * kvw] for g in range(SWA_G)], axis=0)
    qall = (qall * (SWA_HD ** -0.5 * LOG2E)).astype(BF16)
    s = _dot_nt(qall, kbd)
    rows = []
    for g in range(SWA_G):
        ps = []
        for kv in range(SWA_KV):
            sink = sink_ref[kv * SWA_G + g] * LOG2E
            sc = s[g * WINDOW:(g + 1) * WINDOW, kv * 2 * WINDOW:(kv + 1) * 2 * WINDOW] + bias_ref[g * SWA_KV + kv]
            m = jnp.maximum(jnp.max(sc, axis=-1, keepdims=True), sink)
            p = jnp.exp2(sc - m)
            den = jnp.sum(p, axis=-1, keepdims=True) + jnp.exp2(sink - m)
            ps.append((p * (1.0 / den)).astype(BF16))
        rows.append(jnp.concatenate(ps, axis=1))
    og = _dot(jnp.concatenate(rows, axis=0), vbd)
    for g in range(SWA_G):
        o_ref[:, g * kvw:(g + 1) * kvw] = og[g * WINDOW:(g + 1) * WINDOW].astype(BF16)


def swa_prompt_attention(qkv, sinks, batch, seq):
    nb = seq // WINDOW
    nq = SWA_HEADS * SWA_HD
    kvw = SWA_KV * SWA_HD
    kcol = nq // kvw
    bias = jnp.asarray(_swa_bias_tables())
    cur = lambda b, n: (b * nb + n, kcol)
    prev = lambda b, n: (b * nb + jnp.maximum(n - 1, 0), kcol)
    cur_v = lambda b, n: (b * nb + n, kcol + 1)
    prev_v = lambda b, n: (b * nb + jnp.maximum(n - 1, 0), kcol + 1)
    return pl.pallas_call(
        _swa_prompt_kernel,
        out_shape=jax.ShapeDtypeStruct((batch * seq, nq), BF16),
        grid=(batch, nb),
        in_specs=[
            pl.BlockSpec(memory_space=pltpu.SMEM),
            pl.BlockSpec((None,) + bias.shape[1:], lambda b, n: (jnp.minimum(n, 1), 0, 0, 0)),
            pl.BlockSpec((WINDOW, nq), lambda b, n: (b * nb + n, 0)),
            pl.BlockSpec((WINDOW, kvw), cur),
            pl.BlockSpec((WINDOW, kvw), prev),
            pl.BlockSpec((WINDOW, kvw), cur_v),
            pl.BlockSpec((WINDOW, kvw), prev_v),
        ],
        out_specs=pl.BlockSpec((WINDOW, nq), lambda b, n: (b * nb + n, 0)),
        compiler_params=_params(("parallel", "parallel")),
        name="swa_prompt",
    )(sinks, bias, qkv, qkv, qkv, qkv, qkv)


def _swa_sample_kernel(dec_t, sink_ref, q_ref, kc_ref, vc_ref, kn_ref, vn_ref, o_ref):
    nrow = SWA_G * dec_t
    scale = SWA_HD ** -0.5
    r = lax.broadcasted_iota(I32, (nrow, 2 * WINDOW), 0)
    j = lax.broadcasted_iota(I32, (nrow, 2 * WINDOW), 1)
    tq = _imod(r, dec_t)
    gq = _idiv(r, dec_t)
    dist = WINDOW + tq - j
    valid = (dist >= 0) & (dist < WINDOW)
    distf = dist.astype(F32)
    g1 = _idiv(lax.broadcasted_iota(I32, (nrow, 1), 0), dec_t)
    npad = 2 * WINDOW - WINDOW - kn_ref.shape[1]
    zpad = jnp.zeros((npad, SWA_KV * SWA_HD), F32)
    for s in range(q_ref.shape[0]):
        kall = jnp.concatenate([kc_ref[s], kn_ref[s], zpad], axis=0).astype(BF16)
        vall = jnp.concatenate([vc_ref[s], vn_ref[s], zpad], axis=0).astype(BF16)
        kbd = _block_diag(kall, SWA_KV)
        vbd = _block_diag(vall, SWA_KV)
        sc_all = _dot_nt(q_ref[s].astype(BF16), kbd)
        ps = []
        for kv in range(SWA_KV):
            slope = jnp.zeros((nrow, 2 * WINDOW), F32)
            sink = jnp.zeros((nrow, 1), F32)
            for g in range(SWA_G):
                slope = jnp.where(gq == g, _slope(kv, g), slope)
                sink = jnp.where(g1 == g, sink_ref[kv * SWA_G + g], sink)
            sc = sc_all[:, kv * 2 * WINDOW:(kv + 1) * 2 * WINDOW] * scale - slope * distf
            sc = jnp.where(valid, sc, NEG_INF)
            m = jnp.maximum(jnp.max(sc, axis=-1, keepdims=True), sink)
            p = jnp.exp(sc - m)
            den = jnp.sum(p, axis=-1, keepdims=True) + jnp.exp(sink - m)
            ps.append((p / den).astype(BF16))
        o_ref[s] = _dot(jnp.concatenate(ps, axis=1), vbd).astype(BF16)


def swa_sample_attention(q_s, k_new, v_new, cache_k, cache_v, sinks, dec_t):
    nbatch, nrow, kvw = q_s.shape
    sb = _tile(nbatch, SWA_SAMPLE_SEQS)
    blk3 = lambda r: pl.BlockSpec((sb, r, kvw), lambda i: (i, 0, 0))
    return pl.pallas_call(
        functools.partial(_swa_sample_kernel, dec_t),
        out_shape=jax.ShapeDtypeStruct((nbatch, nrow, kvw), BF16),
        grid=(nbatch // sb,),
        in_specs=[
            pl.BlockSpec(memory_space=pltpu.SMEM),
            blk3(nrow), blk3(WINDOW), blk3(WINDOW), blk3(k_new.shape[1]), blk3(v_new.shape[1]),
        ],
        out_specs=blk3(nrow),
        compiler_params=_params(("parallel",)),
        name="swa_sample",
    )(sinks, q_s, cache_k, cache_v, k_new, v_new)


def _proj_res_kernel(n_prompt_tiles, split_x, *refs):
    if split_x:
        xp_ref, xs_ref, op_ref, os_ref, w_ref, b_ref, xo_ref, wbf_ref = refs
    else:
        x_ref, op_ref, os_ref, w_ref, b_ref, xo_ref, wbf_ref = refs
    i = pl.program_id(0)

    @pl.when(i == 0)
    def _():
        wbf_ref[...] = w_ref[...].astype(BF16)

    is_prompt = i < n_prompt_tiles
    x = jnp.where(is_prompt, xp_ref[...], xs_ref[...]) if split_x else x_ref[...]
    o = jnp.where(is_prompt, op_ref[...], os_ref[...])
    xo_ref[...] = x + (_dot(o, wbf_ref[...]) + b_ref[...])


def proj_residual(x, o_prompt, o_sample, w, b):
    split_x = isinstance(x, tuple)
    k, d = w.shape
    t = o_prompt.shape[0] + o_sample.shape[0]
    tm = _tile(o_sample.shape[0], 512)
    assert o_prompt.shape[0] % tm == 0
    npt = o_prompt.shape[0] // tm
    x_specs = list(_split_specs(tm, d, npt)) if split_x else [pl.BlockSpec((tm, d), lambda i: (i, 0))]
    x_args = list(x) if split_x else [x]
    return pl.pallas_call(
        functools.partial(_proj_res_kernel, npt, split_x),
        out_shape=jax.ShapeDtypeStruct((t, d), F32),
        grid=(t // tm,),
        in_specs=x_specs + [
            *_split_specs(tm, k, npt),
            pl.BlockSpec((k, d), lambda i: (0, 0)),
            pl.BlockSpec((1, d), lambda i: (0, 0)),
        ],
        out_specs=pl.BlockSpec((tm, d), lambda i: (i, 0)),
        scratch_shapes=[pltpu.VMEM((k, d), BF16)],
        compiler_params=_params(("arbitrary",)),
        name="proj_residual",
    )(*x_args, o_prompt, o_sample, w, b.reshape(1, d))


def _split3_dot(a, w_hi, w_lo):
    a_hi = a.astype(BF16)
    a_lo = (a - a_hi.astype(F32)).astype(BF16)
    return _dot(a_hi, w_hi) + (_dot(a_hi, w_lo) + _dot(a_lo, w_hi))


def _router_kernel(x_ref, g_ref, w_ref, b_ref, f_ref, meta_ref, cnt_ref, whi_ref, wlo_ref, carry_ref):
    i = pl.program_id(0)
    tm = x_ref.shape[0]

    @pl.when(i == 0)
    def _():
        w = w_ref[...]
        hi = w.astype(BF16)
        whi_ref[...] = hi
        wlo_ref[...] = (w - hi.astype(F32)).astype(BF16)
        carry_ref[...] = jnp.zeros_like(carry_ref)

    f = _rms(x_ref[...], g_ref[...])
    f_ref[...] = f
    logits = _split3_dot(f, whi_ref[...], wlo_ref[...]) + b_ref[...]
    lane = lax.broadcasted_iota(I32, (tm, LANES), 1)
    lanef = lane.astype(F32)
    big = float(LANES)

    def first_max(mask):
        v = jnp.max(jnp.where(mask, logits, NEG_INF), axis=-1, keepdims=True)
        idx = jnp.min(jnp.where(mask & (logits == v), lanef, big), axis=-1, keepdims=True)
        return v, idx

    gmask = lane < N_GROUPS
    gmax, gidx = first_max(gmask)
    gsum = jnp.sum(jnp.where(gmask, jnp.exp(logits - gmax), 0.0), axis=-1, keepdims=True)
    g_gate = 1.0 / gsum
    lo = N_GROUPS + EXPERTS_PER_GROUP * gidx
    emask = (lanef >= lo) & (lanef < lo + EXPERTS_PER_GROUP)
    e1, i1 = first_max(emask)
    e2, i2 = first_max(emask & (lanef != i1))
    z = jnp.exp(e2 - e1)
    gate1 = g_gate / (1.0 + z)
    gate2 = g_gate * z / (1.0 + z)
    id1 = i1 - N_GROUPS
    id2 = i2 - N_GROUPS
    oh1 = (lanef == id1).astype(F32)
    oh2 = (lanef == id2).astype(F32)
    oh = oh1 + oh2
    rr = lax.broadcasted_iota(I32, (tm, tm), 0)
    cc = lax.broadcasted_iota(I32, (tm, tm), 1)
    tri = (cc < rr).astype(BF16)
    before = _dot(tri, oh.astype(BF16)) + carry_ref[...]
    rank1 = jnp.sum(oh1 * before, axis=-1, keepdims=True)
    rank2 = jnp.sum(oh2 * before, axis=-1, keepdims=True)
    carry_ref[...] = carry_ref[...] + jnp.sum(oh, axis=0, keepdims=True)
    cnt_ref[...] = carry_ref[...]
    meta = jnp.zeros((tm, LANES), F32)
    for k, v in enumerate((id1, id2, gate1, gate2, rank1, rank2)):
        meta = jnp.where(lane == k, v, meta)
    meta_ref[...] = meta


def moe_router(x, g, w_group, b_group, w_expert, b_expert):
    t, d = x.shape
    tm = _tile(t, 512)
    w = jnp.zeros((d, LANES), F32).at[:, :N_GROUPS].set(w_group).at[:, N_GROUPS:N_GROUPS + N_EXPERTS].set(w_expert)
    b = jnp.zeros((1, LANES), F32).at[0, :N_GROUPS].set(b_group).at[0, N_GROUPS:N_GROUPS + N_EXPERTS].set(b_expert)
    return pl.pallas_call(
        _router_kernel,
        out_shape=(
            jax.ShapeDtypeStruct((t, d), F32),
            jax.ShapeDtypeStruct((t, LANES), F32),
            jax.ShapeDtypeStruct((1, LANES), F32),
        ),
        grid=(t // tm,),
        in_specs=[
            pl.BlockSpec((tm, d), lambda i: (i, 0)),
            pl.BlockSpec((1, d), lambda i: (0, 0)),
            pl.BlockSpec((d, LANES), lambda i: (0, 0)),
            pl.BlockSpec((1, LANES), lambda i: (0, 0)),
        ],
        out_specs=(
            pl.BlockSpec((tm, d), lambda i: (i, 0)),
            pl.BlockSpec((tm, LANES), lambda i: (i, 0)),
            pl.BlockSpec((1, LANES), lambda i: (0, 0)),
        ),
        scratch_shapes=[pltpu.VMEM((d, LANES), BF16), pltpu.VMEM((d, LANES), BF16), pltpu.VMEM((1, LANES), F32)],
        compiler_params=_params(("arbitrary",)),
        name="moe_router",
    )(x, g.reshape(1, d), w, b)


def _dest_kernel(meta_ref, ps_ref, d_ref):
    meta = meta_ref[...]
    tm = meta.shape[0]
    lane = lax.broadcasted_iota(I32, (tm, LANES), 1)
    lanef = lane.astype(F32)
    ps = ps_ref[...]
    both = jnp.zeros((tm, LANES), F32)
    for k in range(TOP_K):
        start = jnp.sum(jnp.where(lanef == meta[:, k:k + 1], ps, 0.0), axis=-1, keepdims=True)
        both = jnp.where(lane == k, start + meta[:, 4 + k:5 + k], both)
    d_ref[0] = both.T[:8, :].astype(I32)


def moe_dest(meta, pstart):
    t = meta.shape[0]
    tm = _tile(t, COMBINE_TILE)
    return pl.pallas_call(
        _dest_kernel,
        out_shape=jax.ShapeDtypeStruct((t // tm, 8, tm), I32),
        grid=(t // tm,),
        in_specs=[pl.BlockSpec((tm, LANES), lambda i: (i, 0)), pl.BlockSpec((1, LANES), lambda i: (0, 0))],
        out_specs=pl.BlockSpec((1, 8, tm), lambda i: (i, 0, 0)),
        compiler_params=_params(("parallel",)),
        name="moe_dest",
    )(meta, pstart)


def _dispatch_kernel(zblk_ref, dcur_ref, dprev_ref, f_hbm, xs_hbm, zero_buf, ring, lsem, sem, zsem):
    i = pl.program_id(0)
    n = pl.num_programs(0)
    tm = dcur_ref.shape[2]
    blk = zero_buf.shape[0]

    nblocks = xs_hbm.shape[0] // blk
    nused = zblk_ref[N_EXPERTS]

    def zero_copy(b):
        return pltpu.make_async_copy(zero_buf, xs_hbm.at[pl.ds(b * blk, blk)], zsem)

    @pl.when(i == 0)
    def _():
        zero_buf[...] = jnp.zeros_like(zero_buf)

        def zero_blocks(wait):
            def last_of_expert(e, carry):
                cp = zero_copy(zblk_ref[e])
                cp.wait() if wait else cp.start()
                return carry

            def unused(b, carry):
                cp = zero_copy(b)
                cp.wait() if wait else cp.start()
                return carry

            lax.fori_loop(0, N_EXPERTS, last_of_expert, 0)
            lax.fori_loop(nused, nblocks, unused, 0)

        zero_blocks(False)
        zero_blocks(True)

    def load(tile):
        slot = tile % 3
        return pltpu.make_async_copy(f_hbm.at[pl.ds(tile * tm, tm)], ring.at[slot], lsem.at[slot])

    def scatter(d_ref, tile, wait):
        slot = tile % 3

        def body(r, carry):
            for k in range(TOP_K):
                cp = pltpu.make_async_copy(ring.at[slot, pl.ds(r, 1)], xs_hbm.at[pl.ds(d_ref[0, k, r], 1)],
                                           sem.at[slot])
                cp.wait() if wait else cp.start(priority=k)
            return carry
        lax.fori_loop(0, tm, body, 0, unroll=8)

    @pl.when(i == 0)
    def _():
        load(0).start()

    @pl.when(i + 1 < n)
    def _():
        load(i + 1).start()

    load(i).wait()
    scatter(dcur_ref, i, False)

    @pl.when(i > 0)
    def _():
        scatter(dprev_ref, i - 1, True)

    @pl.when(i == n - 1)
    def _():
        scatter(dcur_ref, i, True)


def moe_dispatch(f, dest, zblk, nslot):
    t, d = f.shape
    nt, _, tm = dest.shape
    grid_spec = pltpu.PrefetchScalarGridSpec(
        num_scalar_prefetch=1,
        grid=(nt,),
        in_specs=[
            pl.BlockSpec((1, 8, tm), lambda i, zb: (i, 0, 0), memory_space=pltpu.SMEM),
            pl.BlockSpec((1, 8, tm), lambda i, zb: (jnp.maximum(i - 1, 0), 0, 0), memory_space=pltpu.SMEM),
            pl.BlockSpec(memory_space=pl.ANY),
        ],
        out_specs=pl.BlockSpec(memory_space=pl.ANY),
        scratch_shapes=[pltpu.VMEM((MOE_BLOCK, d), F32), pltpu.VMEM((3, tm, d), F32),
                        pltpu.SemaphoreType.DMA((3,)), pltpu.SemaphoreType.DMA((3,)), pltpu.SemaphoreType.DMA],
    )
    return pl.pallas_call(
        _dispatch_kernel,
        out_shape=jax.ShapeDtypeStruct((nslot, d), F32),
        grid_spec=grid_spec,
        compiler_params=_params(("arbitrary",)),
        name="moe_dispatch",
    )(zblk, dest, dest, f)


def _expert_kernel(blk_e_ref, nblk_ref, x_ref, wg_ref, wu_ref, wd_ref, y_ref, wg_bf, wu_bf, wd_bf):
    j = pl.program_id(0)
    nb = nblk_ref[0]

    @pl.when(j < nb)
    def _():
        e = blk_e_ref[j]
        e_prev = blk_e_ref[jnp.maximum(j - 1, 0)]

        @pl.when(jnp.logical_or(j == 0, e != e_prev))
        def _():
            wg_bf[...] = wg_ref[...].astype(BF16)
            wu_bf[...] = wu_ref[...].astype(BF16)
            wd_bf[...] = wd_ref[...].astype(BF16)

        x = x_ref[...].astype(BF16)
        a = _dot(x, wg_bf[...])
        u = _dot(x, wu_bf[...])
        h = a * (1.0 / (1.0 + jnp.exp(-a))) * u
        y_ref[...] = _dot(h.astype(BF16), wd_bf[...])

    @pl.when(j >= nb)
    def _():
        y_ref[...] = jnp.zeros_like(y_ref)


def moe_experts(xs, blk_e, nblk, layer, w_gate, w_up, w_down):
    nslot, d = xs.shape
    nblocks = nslot // MOE_BLOCK
    de = w_gate.shape[3]
    wmap = lambda j, be, nb: (layer, be[j], 0, 0)
    grid_spec = pltpu.PrefetchScalarGridSpec(
        num_scalar_prefetch=2,
        grid=(nblocks,),
        in_specs=[
            pl.BlockSpec((MOE_BLOCK, d), lambda j, be, nb: (jnp.minimum(j, nb[0] - 1), 0)),
            pl.BlockSpec((None, None, d, de), wmap),
            pl.BlockSpec((None, None, d, de), wmap),
            pl.BlockSpec((None, None, de, d), wmap),
        ],
        out_specs=pl.BlockSpec((MOE_BLOCK, d), lambda j, be, nb: (j, 0)),
        scratch_shapes=[pltpu.VMEM((d, de), BF16), pltpu.VMEM((d, de), BF16), pltpu.VMEM((de, d), BF16)],
    )
    return pl.pallas_call(
        _expert_kernel,
        out_shape=jax.ShapeDtypeStruct((nslot, d), F32),
        grid_spec=grid_spec,
        compiler_params=_params(("arbitrary",)),
        name="moe_experts",
    )(blk_e, nblk, xs, w_gate, w_up, w_down)


def _combine_kernel(n_prompt_tiles, dcur_ref, dnxt_ref, yb_hbm, x_ref, meta_ref, g_ref, *refs):
    final_norm = n_prompt_tiles is not None
    if final_norm:
        op_ref, os_ref, ybuf, sem = refs
    else:
        o_ref, ybuf, sem = refs
    i = pl.program_id(0)
    n = pl.num_programs(0)
    tm = x_ref.shape[0]

    def gather(d_ref, slot, wait):
        def body(r, carry):
            for k in range(TOP_K):
                cp = pltpu.make_async_copy(yb_hbm.at[pl.ds(d_ref[0, k, r], 1)], ybuf.at[slot, k, pl.ds(r, 1)],
                                           sem.at[slot])
                cp.wait() if wait else cp.start(priority=k)
            return carry
        lax.fori_loop(0, tm, body, 0, unroll=8)

    @pl.when(i == 0)
    def _():
        gather(dcur_ref, 0, False)

    @pl.when(i + 1 < n)
    def _():
        gather(dnxt_ref, (i + 1) % 2, False)

    slot = i % 2
    gather(dcur_ref, slot, True)
    meta = meta_ref[...]
    gate1 = meta[:, 2:3]
    gate2 = meta[:, 3:4]
    out = x_ref[...] + (ybuf[slot, 0] * gate1 + ybuf[slot, 1] * gate2)
    if final_norm:
        out = _rms(out, g_ref[...])

        @pl.when(i < n_prompt_tiles)
        def _():
            op_ref[...] = out

        @pl.when(i >= n_prompt_tiles)
        def _():
            os_ref[...] = out
    else:
        o_ref[...] = out


def moe_combine(x, yb, dest, meta, g_final, n_prompt):
    t, d = x.shape
    nt, _, tm = dest.shape
    cur = pl.BlockSpec((1, 8, tm), lambda i: (i, 0, 0), memory_space=pltpu.SMEM)
    nxt = pl.BlockSpec((1, 8, tm), lambda i: (jnp.minimum(i + 1, nt - 1), 0, 0), memory_space=pltpu.SMEM)
    if n_prompt is None:
        npt = None
        out_shape = jax.ShapeDtypeStruct((t, d), F32)
        out_specs = pl.BlockSpec((tm, d), lambda i: (i, 0))
    else:
        assert n_prompt % tm == 0
        npt = n_prompt // tm
        out_shape = (jax.ShapeDtypeStruct((n_prompt, d), F32), jax.ShapeDtypeStruct((t - n_prompt, d), F32))
        out_specs = _split_specs(tm, d, npt)
    return pl.pallas_call(
        functools.partial(_combine_kernel, npt),
        out_shape=out_shape,
        grid=(nt,),
        in_specs=[
            cur, nxt,
            pl.BlockSpec(memory_space=pl.ANY),
            pl.BlockSpec((tm, d), lambda i: (i, 0)),
            pl.BlockSpec((tm, LANES), lambda i: (i, 0)),
            pl.BlockSpec((1, d), lambda i: (0, 0)),
        ],
        out_specs=out_specs,
        scratch_shapes=[pltpu.VMEM((2, TOP_K, tm, d), F32), pltpu.SemaphoreType.DMA((2,))],
        compiler_params=_params(("arbitrary",)),
        name="moe_combine",
    )(dest, dest, yb, x, meta, g_final.reshape(1, d))


def hier_moe_layer(x, layer, norm_g, w_group, b_group, w_expert, b_expert, w_gate, w_up, w_down, g_final,
                   n_prompt):
    t, d = x.shape
    f, meta, counts = moe_router(x, norm_g, w_group, b_group, w_expert, b_expert)
    cnt = counts[0, :N_EXPERTS].astype(I32)
    padded = (cnt + MOE_BLOCK - 1) // MOE_BLOCK * MOE_BLOCK
    pend = jnp.cumsum(padded)
    pstart = pend - padded
    nblocks = -(-(t * TOP_K) // MOE_BLOCK) + N_EXPERTS
    blk_start = jnp.arange(nblocks, dtype=I32) * MOE_BLOCK
    blk_e = jnp.minimum(jnp.sum((pend[None, :] <= blk_start[:, None]).astype(I32), axis=1), N_EXPERTS - 1)
    nblk = (pend[-1:] // MOE_BLOCK).astype(I32)
    zblk = jnp.concatenate([jnp.maximum(pend // MOE_BLOCK - 1, 0), pend[-1:] // MOE_BLOCK]).astype(I32)
    ps_row = jnp.concatenate([pstart.astype(F32), jnp.zeros((LANES - N_EXPERTS,), F32)]).reshape(1, LANES)
    dest = moe_dest(meta, ps_row)
    xs = moe_dispatch(f, dest, zblk, nblocks * MOE_BLOCK)
    yb = moe_experts(xs, blk_e, nblk, layer, w_gate, w_up, w_down)
    return moe_combine(x, yb, dest, meta, g_final, n_prompt)


def _mla_down_kernel(x_ref, g_ref, w_ref, gq_ref, gkv_ref, cos_ref, sin_ref, cq_ref, ckr_ref, wbf_ref):
    @pl.when(pl.program_id(0) == 0)
    def _():
        wbf_ref[...] = w_ref[...].astype(BF16)

    h = _rms(x_ref[...], g_ref[...])
    a = _dot(h.astype(BF16), wbf_ref[...])
    q0, c0 = MLA_Q_LORA, MLA_Q_LORA + MLA_KV_LORA
    cq_ref[...] = _rms(a[:, :q0], gq_ref[...]).astype(BF16)
    ckr_ref[:, :MLA_KV_LORA] = _rms(a[:, q0:c0], gkv_ref[...])
    ckr_ref[:, MLA_KV_LORA:] = a[:, c0:c0 + LANES] * cos_ref[...] + a[:, c0 + LANES:] * sin_ref[...]


def _rot_cols(w):
    half = MLA_ROPE // 2
    return jnp.concatenate([-w[..., half:], w[..., :half]], axis=-1)


def mla_down(x, g, w_dqkv, norm_q, norm_kv, cos_t, sin_t):
    t, d = x.shape
    tm = _tile(t, 512)
    q0, c0 = MLA_Q_LORA, MLA_Q_LORA + MLA_KV_LORA
    w_r = w_dqkv[:, c0:]
    zpad = jnp.zeros((d, LANES - MLA_ROPE), F32)
    w = jnp.concatenate([w_dqkv[:, :c0], w_r, zpad, _rot_cols(w_r), zpad], axis=1)
    n = w.shape[1]
    return pl.pallas_call(
        _mla_down_kernel,
        out_shape=(jax.ShapeDtypeStruct((t, q0), BF16), jax.ShapeDtypeStruct((t, MLA_KV_LORA + LANES), F32)),
        grid=(t // tm,),
        in_specs=[
            pl.BlockSpec((tm, d), lambda i: (i, 0)),
            pl.BlockSpec((1, d), lambda i: (0, 0)),
            pl.BlockSpec((d, n), lambda i: (0, 0)),
            pl.BlockSpec((1, q0), lambda i: (0, 0)),
            pl.BlockSpec((1, MLA_KV_LORA), lambda i: (0, 0)),
            pl.BlockSpec((tm, LANES), lambda i: (i, 0)),
            pl.BlockSpec((tm, LANES), lambda i: (i, 0)),
        ],
        out_specs=(pl.BlockSpec((tm, q0), lambda i: (i, 0)),
                   pl.BlockSpec((tm, MLA_KV_LORA + LANES), lambda i: (i, 0))),
        scratch_shapes=[pltpu.VMEM((d, n), BF16)],
        compiler_params=_params(("arbitrary",)),
        name="mla_down",
    )(x, g.reshape(1, d), w, norm_q.reshape(1, q0), norm_kv.reshape(1, MLA_KV_LORA), cos_t, sin_t)


def _mla_up_kernel(cq_ref, ckr_ref, qc_ref, qs_ref, wq_ref, wk_ref, wv_ref, q_ref, k_ref, v_ref,
                   wq_bf, wk_bf, wv_bf):
    @pl.when(pl.program_id(0) == 0)
    def _():
        wq_bf[...] = wq_ref[...].astype(BF16)
        wk_bf[...] = wk_ref[...].astype(BF16)
        wv_bf[...] = wv_ref[...].astype(BF16)

    nq = q_ref.shape[1]
    ab = _dot(cq_ref[...], wq_bf[...])
    qc = qc_ref[...]
    qs = qs_ref[...]
    for h in range(MLA_HEADS):
        sl = slice(h * LANES, (h + 1) * LANES)
        sl_b = slice(nq + h * LANES, nq + (h + 1) * LANES)
        q_ref[:, sl] = (ab[:, sl] * qc + ab[:, sl_b] * qs).astype(BF16)
    ckr = ckr_ref[...].astype(BF16)
    k_ref[...] = _dot(ckr, wk_bf[...]).astype(BF16)
    v_ref[...] = _dot(ckr[:, :MLA_KV_LORA], wv_bf[...]).astype(BF16)


def _head_slots(w, lo):
    r, h, n = w.shape
    out = jnp.zeros((r, h, LANES), F32).at[:, :, lo:lo + n].set(w)
    return out.reshape(r, h * LANES)


def mla_up(cq, ckr, q_cos, q_sin, w_uq, w_uk, w_uv):
    t = cq.shape[0]
    tm = _tile(t, 256)
    nq = MLA_HEADS * LANES
    kin = MLA_KV_LORA + LANES
    wq3 = w_uq.reshape(MLA_Q_LORA, MLA_HEADS, MLA_NOPE + MLA_ROPE)
    wq_a = _head_slots(wq3, 0)
    wq_b = _head_slots(_rot_cols(wq3[:, :, MLA_NOPE:]), MLA_NOPE)
    wq = jnp.concatenate([wq_a, wq_b], axis=1)
    place = jnp.zeros((LANES, MLA_HEADS, LANES), F32)
    place = place.at[jnp.arange(MLA_ROPE), :, MLA_NOPE + jnp.arange(MLA_ROPE)].set(1.0)
    wk = jnp.concatenate([_head_slots(w_uk, 0), place.reshape(LANES, nq)], axis=0)
    even = (jnp.arange(MLA_HEADS) % 2 == 0)[None, :, None]
    wv = jnp.where(even, _head_slots(w_uv, 0).reshape(MLA_KV_LORA, MLA_HEADS, LANES),
                   _head_slots(w_uv, MLA_V).reshape(MLA_KV_LORA, MLA_HEADS, LANES)).reshape(MLA_KV_LORA, nq)
    full = lambda r, c: pl.BlockSpec((r, c), lambda i: (0, 0))
    rows = lambda c: pl.BlockSpec((tm, c), lambda i: (i, 0))
    return pl.pallas_call(
        _mla_up_kernel,
        out_shape=tuple(jax.ShapeDtypeStruct((t, nq), BF16) for _ in range(3)),
        grid=(t // tm,),
        in_specs=[rows(MLA_Q_LORA), rows(kin), rows(LANES), rows(LANES),
                  full(MLA_Q_LORA, 2 * nq), full(kin, nq), full(MLA_KV_LORA, nq)],
        out_specs=(rows(nq), rows(nq), rows(nq)),
        scratch_shapes=[pltpu.VMEM((MLA_Q_LORA, 2 * nq), BF16), pltpu.VMEM((kin, nq), BF16),
                        pltpu.VMEM((MLA_KV_LORA, nq), BF16)],
        compiler_params=_params(("arbitrary",)),
        name="mla_up",
    )(cq, ckr, q_cos, q_sin, wq, wk, wv)


def _mla_flash_kernel(q_ref, k_ref, v_ref, o_ref, m_ref, l_ref, acc_ref):
    qi = pl.program_id(2)
    tq = q_ref.shape[0]
    m_ref[...] = jnp.full_like(m_ref, NEG_INF)
    l_ref[...] = jnp.zeros_like(l_ref)
    acc_ref[...] = jnp.zeros_like(acc_ref)

    def step(r0, nr, start, nk, masked):
        rows = slice(r0, r0 + nr)
        first_half = lax.broadcasted_iota(I32, (nr, LANES), 1) < MLA_V
        kblk = k_ref[pl.ds(start, nk), :]
        vblk = v_ref[pl.ds(start, nk), :]
        pv = None
        alphas = []
        for h in range(2):
            s = _dot_nt(q_ref[rows, h * LANES:(h + 1) * LANES], kblk[:, h * LANES:(h + 1) * LANES])
            if masked:
                row = lax.broadcasted_iota(I32, s.shape, 0)
                col = lax.broadcasted_iota(I32, s.shape, 1)
                s = jnp.where(col <= row, s, NEG_INF)
            sc = [s[:, c * LANES:(c + 1) * LANES] for c in range(nk // LANES)]
            m_old = m_ref[h, rows]
            m_new = jnp.maximum(m_old, jnp.max(functools.reduce(jnp.maximum, sc), axis=-1, keepdims=True))
            alpha = jnp.exp2(m_old - m_new)
            pc = [jnp.exp2(x - m_new) for x in sc]
            l_ref[h, rows] = alpha * l_ref[h, rows] + functools.reduce(jnp.add, pc)
            m_ref[h, rows] = m_new
            p = jnp.concatenate([x.astype(BF16) for x in pc], axis=1)
            part = _dot(p, vblk[:, h * LANES:(h + 1) * LANES])
            pv = part if pv is None else pv + part
            alphas.append(alpha)
        acc_ref[rows] = jnp.where(first_half, alphas[0], alphas[1]) * acc_ref[rows] + pv

    def body(kb, carry):
        step(0, tq, pl.multiple_of(kb * tq, tq), tq, False)
        return carry

    lax.fori_loop(0, qi, body, 0)
    base = pl.multiple_of(qi * tq, tq)
    half = tq // 2
    step(0, half, base, half, True)
    step(half, half, base, half, False)
    step(half, half, base + half, half, True)
    first_half = lax.broadcasted_iota(I32, (tq, LANES), 1) < MLA_V
    l0 = jnp.sum(l_ref[0], axis=-1, keepdims=True)
    l1 = jnp.sum(l_ref[1], axis=-1, keepdims=True)
    o_ref[...] = (acc_ref[...] * jnp.where(first_half, 1.0 / l0, 1.0 / l1)).astype(BF16)


def mla_prompt_attention(qp, kp, vp, batch, seq):
    tq = _tile(seq, MLA_TQ)
    nqb = seq // tq
    npair = MLA_HEADS // 2
    pw = 2 * LANES
    return pl.pallas_call(
        _mla_flash_kernel,
        out_shape=jax.ShapeDtypeStruct((batch * seq, MLA_HEADS * MLA_V), BF16),
        grid=(batch, npair, nqb),
        in_specs=[
            pl.BlockSpec((tq, pw), lambda b, h, i: (b * nqb + i, h)),
            pl.BlockSpec((seq, pw), lambda b, h, i: (b, h)),
            pl.BlockSpec((seq, pw), lambda b, h, i: (b, h)),
        ],
        out_specs=pl.BlockSpec((tq, LANES), lambda b, h, i: (b * nqb + i, h)),
        scratch_shapes=[pltpu.VMEM((2, tq, LANES), F32), pltpu.VMEM((2, tq, LANES), F32),
                        pltpu.VMEM((tq, LANES), F32)],
        compiler_params=_params(("parallel", "parallel", "arbitrary")),
        name="mla_prompt",
    )(qp, kp, vp)


def _mla_absorb_kernel(q_ref, w_ref, o_ref):
    o_ref[0] = _dot(q_ref[...], w_ref[0].astype(BF16)).astype(BF16)


def mla_absorb(q_s, w_uk):
    ns = q_s.shape[0]
    kin = MLA_KV_LORA + LANES
    w = jnp.zeros((MLA_HEADS, LANES, kin), F32)
    w = w.at[:, :MLA_NOPE, :MLA_KV_LORA].set(jnp.transpose(w_uk, (1, 2, 0)))
    w = w.at[:, MLA_NOPE + jnp.arange(MLA_ROPE), MLA_KV_LORA + jnp.arange(MLA_ROPE)].set(1.0)
    return pl.pallas_call(
        _mla_absorb_kernel,
        out_shape=jax.ShapeDtypeStruct((MLA_HEADS, ns, kin), BF16),
        grid=(MLA_HEADS,),
        in_specs=[pl.BlockSpec((ns, LANES), lambda h: (0, h)),
                  pl.BlockSpec((1, LANES, kin), lambda h: (h, 0, 0))],
        out_specs=pl.BlockSpec((1, ns, kin), lambda h: (h, 0, 0)),
        compiler_params=_params(("parallel",)),
        name="mla_absorb",
    )(q_s, w)


def _mla_decode_kernel(npg, dec_t, pt_ref, q_ref, new_ref, lat_hbm, krt_hbm, o_ref,
                       lat_buf, krt_buf, sem, s_buf, latb_buf):
    b = pl.program_id(0)
    nb = pl.num_programs(0)
    npages = pt_ref.shape[1]
    page = lat_hbm.shape[1]
    ck = npg * page
    nchunk = npages // npg
    nrow = q_ref.shape[0]

    def page_copies(bb, slot, i):
        pg = pt_ref[bb, i]
        return (pltpu.make_async_copy(lat_hbm.at[pg], lat_buf.at[slot, pl.ds(i * page, page)], sem.at[slot]),
                pltpu.make_async_copy(krt_hbm.at[pg], krt_buf.at[slot, pl.ds(i * MLA_ROPE, MLA_ROPE)],
                                      sem.at[slot]))

    def fetch(bb, slot, wait):
        def body(i, carry):
            for cp in page_copies(bb, slot, i):
                cp.wait() if wait else cp.start()
            return carry
        lax.fori_loop(0, npages, body, 0, unroll=8)

    @pl.when(b == 0)
    def _():
        fetch(0, 0, False)

    @pl.when(b + 1 < nb)
    def _():
        fetch(b + 1, (b + 1) % 2, False)

    slot = b % 2
    fetch(b, slot, True)

    q = q_ref[...]
    ql = q[:, :MLA_KV_LORA]
    qr = q[:, MLA_KV_LORA:MLA_KV_LORA + MLA_ROPE]
    mloc = jnp.full((nrow, LANES), NEG_INF, F32)
    for c in range(nchunk):
        lat = lat_buf[slot, c * ck:(c + 1) * ck, :].astype(BF16)
        latb_buf[c * ck:(c + 1) * ck, :] = lat
        krt = jnp.concatenate(
            [krt_buf[slot, (c * npg + i) * MLA_ROPE:(c * npg + i + 1) * MLA_ROPE, :] for i in range(npg)],
            axis=1).astype(BF16)
        s = _dot_nt(ql, lat) + _dot(qr, krt)
        s_buf[:, c * ck:(c + 1) * ck] = s
        for j in range(ck // LANES):
            mloc = jnp.maximum(mloc, s[:, j * LANES:(j + 1) * LANES])

    new = new_ref[...].astype(BF16)
    sn = _dot_nt(q, new)
    t_row = _idiv(lax.broadcasted_iota(I32, sn.shape, 0), MLA_HEADS)
    t_col = lax.broadcasted_iota(I32, sn.shape, 1)
    sn = jnp.where(t_col <= t_row, sn, NEG_INF)
    m = jnp.maximum(jnp.max(mloc, axis=-1, keepdims=True), jnp.max(sn, axis=-1, keepdims=True))
    pn = jnp.exp2(sn - m)
    lsum = jnp.zeros((nrow, LANES), F32)
    acc = jnp.zeros((nrow, MLA_KV_LORA), F32)
    for c in range(nchunk):
        pc = [jnp.exp2(s_buf[:, c * ck + j * LANES:c * ck + (j + 1) * LANES] - m) for j in range(ck // LANES)]
        lsum = lsum + functools.reduce(jnp.add, pc)
        p = jnp.concatenate([x.astype(BF16) for x in pc], axis=1)
        acc = acc + _dot(p, latb_buf[c * ck:(c + 1) * ck, :])
    l = jnp.sum(lsum, axis=-1, keepdims=True) + jnp.sum(pn, axis=-1, keepdims=True)
    pnb = pn.astype(BF16).astype(F32)
    cn = new[:, :MLA_KV_LORA].astype(F32)
    for t in range(dec_t):
        acc = acc + pnb[:, t:t + 1] * cn[t:t + 1, :]
    o_ref[...] = (acc / l).astype(BF16)


def mla_decode(qabs, ckr_new, cache_lat, cache_krt, page_table, dec_t):
    nbatch, npages = page_table.shape
    npg = _tile(npages, MLA_PAGES_PER_STEP)
    nrow = dec_t * MLA_HEADS
    kin = MLA_KV_LORA + LANES
    page = cache_lat.shape[1]
    nkeys = npages * page
    grid_spec = pltpu.PrefetchScalarGridSpec(
        num_scalar_prefetch=1,
        grid=(nbatch,),
        in_specs=[pl.BlockSpec((nrow, kin), lambda b, pt: (b, 0)),
                  pl.BlockSpec((None, 8, kin), lambda b, pt: (b, 0, 0)),
                  pl.BlockSpec(memory_space=pl.ANY),
                  pl.BlockSpec(memory_space=pl.ANY)],
        out_specs=pl.BlockSpec((nrow, MLA_KV_LORA), lambda b, pt: (b, 0)),
        scratch_shapes=[pltpu.VMEM((2, nkeys, MLA_KV_LORA), F32),
                        pltpu.VMEM((2, npages * MLA_ROPE, page), F32),
                        pltpu.SemaphoreType.DMA((2,)),
                        pltpu.VMEM((nrow, nkeys), F32),
                        pltpu.VMEM((nkeys, MLA_KV_LORA), BF16)],
    )
    return pl.pallas_call(
        functools.partial(_mla_decode_kernel, npg, dec_t),
        out_shape=jax.ShapeDtypeStruct((nbatch * nrow, MLA_KV_LORA), BF16),
        grid_spec=grid_spec,
        compiler_params=_params(("arbitrary",)),
        name="mla_decode",
    )(page_table, qabs, ckr_new, cache_lat, cache_krt)


def _mla_unabsorb_kernel(ol_ref, w_ref, o_ref):
    r = _dot(ol_ref[...], w_ref[...].astype(BF16))
    row_h = _imod(lax.broadcasted_iota(I32, r.shape, 0), MLA_HEADS)
    col_h = _idiv(lax.broadcasted_iota(I32, r.shape, 1), MLA_V)
    r = jnp.where(row_h == col_h, r, 0.0)
    o_ref[...] = jnp.sum(r.reshape(r.shape[0] // MLA_HEADS, MLA_HEADS, r.shape[1]), axis=1).astype(BF16)


def mla_unabsorb(o_lat, w_uv):
    rows = o_lat.shape[0]
    tr = _tile(rows, 1024)
    n = MLA_HEADS * MLA_V
    return pl.pallas_call(
        _mla_unabsorb_kernel,
        out_shape=jax.ShapeDtypeStruct((rows // MLA_HEADS, n), BF16),
        grid=(rows // tr,),
        in_specs=[pl.BlockSpec((tr, MLA_KV_LORA), lambda i: (i, 0)),
                  pl.BlockSpec((MLA_KV_LORA, n), lambda i: (0, 0))],
        out_specs=pl.BlockSpec((tr // MLA_HEADS, n), lambda i: (i, 0)),
        compiler_params=_params(("parallel",)),
        name="mla_unabsorb",
    )(o_lat, w_uv.reshape(MLA_KV_LORA, n))


def _rope_tables(pos):
    half = MLA_ROPE // 2
    inv = jnp.power(ROPE_THETA, -jnp.arange(half, dtype=F32) * 2.0 / MLA_ROPE)
    ang = pos[:, None] * inv[None, :]
    cos2 = jnp.tile(jnp.cos(ang), (1, 2))
    sin2 = jnp.tile(jnp.sin(ang), (1, 2))
    t = pos.shape[0]
    scale = (MLA_NOPE + MLA_ROPE) ** -0.5 * LOG2E
    zeros = lambda n: jnp.zeros((t, n), F32)
    k_cos = jnp.concatenate([cos2, zeros(LANES - MLA_ROPE)], axis=1)
    k_sin = jnp.concatenate([sin2, zeros(LANES - MLA_ROPE)], axis=1)
    tail = LANES - MLA_NOPE - MLA_ROPE
    q_cos = jnp.concatenate([jnp.full((t, MLA_NOPE), scale, F32), scale * cos2, zeros(tail)], axis=1)
    q_sin = jnp.concatenate([zeros(MLA_NOPE), scale * sin2, zeros(tail)], axis=1)
    return k_cos, k_sin, q_cos, q_sin


def kernel(x_prompt, x_sample, cache_swa_k, cache_swa_v, cache_mla_latent, cache_mla_krope, page_table,
           norm_attn, norm_ffn, norm_final,
           swa_w_qkv, swa_b_qkv, swa_sinks, swa_w_o, swa_b_o,
           mla_w_dqkv, mla_norm_q, mla_norm_kv, mla_w_uq, mla_w_uk, mla_w_uv, mla_w_o,
           moe_w_group, moe_b_group, moe_w_expert, moe_b_expert, moe_w_gate, moe_w_up, moe_w_down):
    batch, seq, d = x_prompt.shape
    nbatch, dec_t, _ = x_sample.shape
    n_p = batch * seq
    n_s = nbatch * dec_t
    npages = page_table.shape[1]
    page = cache_mla_latent.shape[2]
    past_len = npages * page
    nq = SWA_HEADS * SWA_HD
    kvw = SWA_KV * SWA_HD
    x_in = (x_prompt.reshape(n_p, d), x_sample.reshape(n_s, d))

    w_qkv = swa_w_qkv[0]
    w_q = w_qkv[:, :nq].reshape(d, SWA_KV, SWA_G, SWA_HD).transpose(0, 2, 1, 3).reshape(d, nq)
    b_q = swa_b_qkv[0][:nq].reshape(SWA_KV, SWA_G, SWA_HD).transpose(1, 0, 2).reshape(nq)
    w0 = jnp.concatenate([w_q, w_qkv[:, nq:]], axis=1)
    b0 = jnp.concatenate([b_q, swa_b_qkv[0][nq:]])
    w_o0 = swa_w_o[0].reshape(SWA_KV, SWA_G, SWA_HD, d).transpose(1, 0, 2, 3).reshape(nq, d)
    qkv = norm_proj(*x_in, norm_attn[0], w0, b0)
    o_p = swa_prompt_attention(qkv, swa_sinks[0], batch, seq)
    qkv_s = qkv[n_p:]
    q_s = qkv_s[:, :nq].reshape(nbatch, dec_t, SWA_G, kvw).transpose(0, 2, 1, 3).reshape(nbatch, SWA_G * dec_t, kvw)
    k_s = qkv_s[:, nq:nq + kvw].reshape(nbatch, dec_t, kvw)
    v_s = qkv_s[:, nq + kvw:].reshape(nbatch, dec_t, kvw)
    pad8 = lambda a: jnp.pad(a, ((0, 0), (0, 8 - dec_t), (0, 0)))
    ck = cache_swa_k[0].reshape(nbatch, WINDOW, kvw)
    cv = cache_swa_v[0].reshape(nbatch, WINDOW, kvw)
    o_s = swa_sample_attention(q_s, pad8(k_s), pad8(v_s), ck, cv, swa_sinks[0], dec_t)
    o_s = o_s.reshape(nbatch, SWA_G, dec_t, kvw).transpose(0, 2, 1, 3).reshape(n_s, nq)
    x = proj_residual(x_in, o_p, o_s, w_o0, swa_b_o[0])
    x = hier_moe_layer(x, 0, norm_ffn[0], moe_w_group[0], moe_b_group[0], moe_w_expert[0], moe_b_expert[0],
                       moe_w_gate, moe_w_up, moe_w_down, norm_final, None)

    last = jnp.stack([qkv[(b + 1) * seq - WINDOW:(b + 1) * seq, nq:] for b in range(batch)])
    swa_kp = last[:, :, :kvw].reshape(1, batch, WINDOW, SWA_KV, SWA_HD)
    swa_vp = last[:, :, kvw:].reshape(1, batch, WINDOW, SWA_KV, SWA_HD)
    swa_ks = jnp.concatenate([cache_swa_k[0], k_s.reshape(nbatch, dec_t, SWA_KV, SWA_HD)], axis=1)[:, dec_t:][None]
    swa_vs = jnp.concatenate([cache_swa_v[0], v_s.reshape(nbatch, dec_t, SWA_KV, SWA_HD)], axis=1)[:, dec_t:][None]

    pos = jnp.concatenate([jnp.tile(jnp.arange(seq, dtype=F32), batch),
                           jnp.tile(past_len + jnp.arange(dec_t, dtype=F32), nbatch)])
    k_cos, k_sin, q_cos, q_sin = _rope_tables(pos)
    cq, ckr = mla_down(x, norm_attn[1], mla_w_dqkv[0], mla_norm_q[0], mla_norm_kv[0], k_cos, k_sin)
    qp, kp, vp = mla_up(cq, ckr, q_cos, q_sin, mla_w_uq[0], mla_w_uk[0], mla_w_uv[0])
    o_p = mla_prompt_attention(qp, kp, vp, batch, seq)
    qabs = mla_absorb(qp[n_p:], mla_w_uk[0])
    qabs = qabs.transpose(1, 0, 2).reshape(n_s * MLA_HEADS, MLA_KV_LORA + LANES)
    ckr_new = jnp.pad(ckr[n_p:].reshape(nbatch, dec_t, MLA_KV_LORA + LANES), ((0, 0), (0, 8 - dec_t), (0, 0)))
    o_lat = mla_decode(qabs, ckr_new, cache_mla_latent[0], jnp.swapaxes(cache_mla_krope[0], 1, 2), page_table, dec_t)
    o_s = mla_unabsorb(o_lat, mla_w_uv[0])
    x = proj_residual(x, o_p, o_s, mla_w_o[0], jnp.zeros((d,), F32))
    y_p, y_s = hier_moe_layer(x, 1, norm_ffn[1], moe_w_group[1], moe_b_group[1], moe_w_expert[1], moe_b_expert[1],
                              moe_w_gate, moe_w_up, moe_w_down, norm_final, n_p)

    c_all = ckr[:, :MLA_KV_LORA]
    r_all = ckr[:, MLA_KV_LORA:MLA_KV_LORA + MLA_ROPE]
    return (y_p.reshape(batch, seq, d), y_s.reshape(nbatch, dec_t, d),
            swa_kp, swa_vp, swa_ks, swa_vs,
            c_all[:n_p].reshape(1, batch, seq, MLA_KV_LORA), r_all[:n_p].reshape(1, batch, seq, MLA_ROPE),
            c_all[n_p:].reshape(1, nbatch, dec_t, MLA_KV_LORA), r_all[n_p:].reshape(1, nbatch, dec_t, MLA_ROPE))
```
